```python
import math
import jax, jax.numpy as jnp
from jax import lax
import numpy as np

D_MODEL = 4096
BATCH = 1
SEQ = 16384
DEPTH = 2

GRID_W = 64
CTX_LEN = 256
N_EVEN = (DEPTH + 1) // 2
N_ODD = DEPTH // 2
EPS = 1e-6

MLA_HEADS = 16
Q_LORA = 1024
KV_LORA = 512
QK_NOPE = 128
QK_ROPE = 64
V_HEAD = 128
ROPE_THETA = 10000.0
Q_BLOCK = 128
ATTN_SCALE = 1.0 / math.sqrt(QK_NOPE + QK_ROPE)
CONV_CH = 2048
CONV_K = 31
IN_A = Q_LORA + KV_LORA + QK_ROPE + 2 * CONV_CH
MIX_A = MLA_HEADS * V_HEAD + CONV_CH
D_INNER = 2 * D_MODEL
SSM_HEADDIM = 64
SSM_HEADS = D_INNER // SSM_HEADDIM
SSM_GROUPS = 8
D_STATE = 128
SSM_CONV_K = 7
CHUNK = 128
XBC = D_INNER + 2 * SSM_GROUPS * D_STATE
IN_C = D_INNER + XBC + 2 * SSM_HEADS
D_FF = -(-8 * D_MODEL // (3 * 256)) * 256

kernel_name = 'hybrid_mla_conformer_ssd_dit'


def rmsnorm(x, g):
    xf = x.astype(jnp.float32)
    y = xf * lax.rsqrt(jnp.mean(xf * xf, axis=-1, keepdims=True) + EPS)
    return (y * g.astype(jnp.float32)).astype(x.dtype)


def layernorm(x, g, b):
    xf = x.astype(jnp.float32)
    mu = jnp.mean(xf, axis=-1, keepdims=True)
    xc = xf - mu
    y = xc * lax.rsqrt(jnp.mean(xc * xc, axis=-1, keepdims=True) + EPS)
    return (y * g.astype(jnp.float32) + b.astype(jnp.float32)).astype(x.dtype)


def adaln(cond, w, b):
    m = jax.nn.silu(cond) @ w + b
    return jnp.split(m[..., None, :], 6, axis=-1)


def modulate(x, g, shift, scale):
    return rmsnorm(x, g) * (1.0 + scale) + shift


def swiglu(h, w1, w3, w2):
    return (jax.nn.silu(h @ w1) * (h @ w3)) @ w2


def depthwise_conv(x, w, b):
    k = w.shape[0]
    y = lax.conv_general_dilated(x, w[:, None, :].astype(x.dtype), window_strides=(1,),
                                 padding=[(k // 2, k // 2)], dimension_numbers=('NWC', 'WIO', 'NWC'),
                                 feature_group_count=x.shape[-1])
    return y + b


def axial_rope_tables(rows):
    row = jnp.repeat(jnp.arange(rows, dtype=jnp.float32), GRID_W)
    col = jnp.tile(jnp.arange(GRID_W, dtype=jnp.float32), rows)
    quarter = QK_ROPE // 4
    freqs = 1.0 / (ROPE_THETA ** (jnp.arange(quarter, dtype=jnp.float32) / quarter))
    ar = row[:, None] * freqs
    ac = col[:, None] * freqs
    ang = jnp.concatenate([ar, ar, ac, ac], axis=-1)
    return jnp.cos(ang), jnp.sin(ang)


def apply_rope(x, cos, sin):
    xr = x.reshape(x.shape[:-1] + (2, 2, QK_ROPE // 4))
    rot = jnp.stack([-xr[..., 1, :], xr[..., 0, :]], axis=-2).reshape(x.shape)
    return (x * cos + rot * sin).astype(x.dtype)


def block_attention(q_nope, q_rope, k_nope, k_rope, v):
    b, s, h, _ = q_nope.shape
    nb = s // Q_BLOCK

    def blk(args):
        qn, qr = args
        sc = (jnp.einsum('bqhd,bkhd->bhqk', qn, k_nope, preferred_element_type=jnp.float32)
              + jnp.einsum('bqhr,bkr->bhqk', qr, k_rope, preferred_element_type=jnp.float32)) * ATTN_SCALE
        p = jax.nn.softmax(sc, axis=-1).astype(v.dtype)
        return jnp.einsum('bhqk,bkhd->bqhd', p, v)

    qn_b = jnp.moveaxis(q_nope.reshape(b, nb, Q_BLOCK, h, QK_NOPE), 1, 0)
    qr_b = jnp.moveaxis(q_rope.reshape(b, nb, Q_BLOCK, h, QK_ROPE), 1, 0)
    out = lax.map(blk, (qn_b, qr_b))
    return jnp.moveaxis(out, 0, 1).reshape(b, s, h * V_HEAD)


def mla_conv_branches(u, g_q, w_uq, g_kv, w_ukv, conv_w, conv_b, ln_g, ln_b):
    b, l, _ = u.shape
    cq, ckv, kr, pw = jnp.split(u, [Q_LORA, Q_LORA + KV_LORA, Q_LORA + KV_LORA + QK_ROPE], axis=-1)
    q = (rmsnorm(cq, g_q) @ w_uq).reshape(b, l, MLA_HEADS, QK_NOPE + QK_ROPE)
    kv = (rmsnorm(ckv, g_kv) @ w_ukv).reshape(b, l, MLA_HEADS, QK_NOPE + V_HEAD)
    q_nope, q_rope = q[..., :QK_NOPE], q[..., QK_NOPE:]
    k_nope, v = kv[..., :QK_NOPE], kv[..., QK_NOPE:]
    gl = pw[..., :CONV_CH] * jax.nn.sigmoid(pw[..., CONV_CH:])
    gl = jax.nn.silu(layernorm(depthwise_conv(gl, conv_w, conv_b), ln_g, ln_b))
    return q_nope, q_rope, k_nope, kr, v, gl


def attn_conv_mixer(hx, hc, w_in, g_q, w_uq, g_kv, w_ukv, conv_w, conv_b, ln_g, ln_b, w_o, cos, sin, need_ctx):
    qn_x, qr_x, kn_x, kr_x, v_x, cv_x = mla_conv_branches(hx @ w_in, g_q, w_uq, g_kv, w_ukv, conv_w, conv_b, ln_g, ln_b)
    qn_c, qr_c, kn_c, kr_c, v_c, cv_c = mla_conv_branches(hc @ w_in, g_q, w_uq, g_kv, w_ukv, conv_w, conv_b, ln_g, ln_b)
    qr_x = apply_rope(qr_x, cos[:, None, :], sin[:, None, :])
    kr_x = apply_rope(kr_x, cos, sin)
    o_x = block_attention(qn_x, qr_x, jnp.concatenate([kn_x, kn_c], axis=1),
                          jnp.concatenate([kr_x, kr_c], axis=1), jnp.concatenate([v_x, v_c], axis=1))
    out_x = jnp.concatenate([o_x, cv_x], axis=-1) @ w_o
    if not need_ctx:
        return out_x, None
    o_c = block_attention(qn_c, qr_c, kn_c, kr_c, v_c)
    return out_x, jnp.concatenate([o_c, cv_c], axis=-1) @ w_o


def ssd_scan(x, dt, a, bm, cm, h0):
    b, l, h, p = x.shape
    g, n = bm.shape[2], bm.shape[3]
    hg = h // g
    nc = l // CHUNK

    def to_chunks(t):
        return t.astype(jnp.float32).reshape((b, nc, CHUNK) + t.shape[2:]).swapaxes(0, 1)

    xs = (to_chunks(x).reshape(nc, b, CHUNK, g, hg, p), to_chunks(dt).reshape(nc, b, CHUNK, g, hg),
          to_chunks(bm), to_chunks(cm))
    a = a.astype(jnp.float32).reshape(g, hg)
    lower = jnp.tril(jnp.ones((CHUNK, CHUNK), dtype=bool))[None, :, :, None, None]

    def step(state, inp):
        xc, dtc, bc, cc = inp
        acum = jnp.cumsum(dtc * a, axis=1)
        seg = acum[:, :, None] - acum[:, None]
        decay = jnp.exp(jnp.where(lower, seg, -jnp.inf))
        cb = jnp.einsum('bign,bjgn->bijg', cc, bc)
        mix = cb[..., None] * decay * dtc[:, None]
        y = jnp.einsum('bijgh,bjghp->bighp', mix, xc)
        y = y + jnp.einsum('bign,bghpn->bighp', cc, state) * jnp.exp(acum)[..., None]
        last = acum[:, -1]
        wgt = jnp.exp(last[:, None] - acum) * dtc
        state = state * jnp.exp(last)[..., None, None] + jnp.einsum('bjghp,bjgn->bghpn', wgt[..., None] * xc, bc)
        return state, y

    state, ys = lax.scan(step, h0.reshape(b, g, hg, p, n), xs)
    y = ys.swapaxes(0, 1).reshape(b, l, h, p)
    return y.astype(x.dtype), state.reshape(b, h, p, n)


def ssd_branch(u, conv_w, conv_b, dt_bias, a_log, d_skip, norm_g, h0_f, h0_b):
    b, l, _ = u.shape
    z, xbc, dt = jnp.split(u, [D_INNER, D_INNER + XBC], axis=-1)
    xbc = jax.nn.silu(depthwise_conv(xbc, conv_w, conv_b))
    xs, bm, cm = jnp.split(xbc, [D_INNER, D_INNER + SSM_GROUPS * D_STATE], axis=-1)
    xs = xs.reshape(b, l, SSM_HEADS, SSM_HEADDIM)
    bm = bm.reshape(b, l, SSM_GROUPS, D_STATE)
    cm = cm.reshape(b, l, SSM_GROUPS, D_STATE)
    dt = jax.nn.softplus(dt.astype(jnp.float32).reshape(b, l, 2, SSM_HEADS) + dt_bias.astype(jnp.float32))
    a = -jnp.exp(a_log.astype(jnp.float32))
    y_f, s_f = ssd_scan(xs, dt[:, :, 0], a[0], bm, cm, h0_f)
    y_b, s_b = ssd_scan(jnp.flip(xs, 1), jnp.flip(dt[:, :, 1], 1), a[1], jnp.flip(bm, 1), jnp.flip(cm, 1), h0_b)
    y = y_f + jnp.flip(y_b, 1) + d_skip[:, None] * xs
    y = rmsnorm(y.reshape(b, l, D_INNER) * jax.nn.silu(z), norm_g)
    return y, s_f, s_b


def ssd_mixer(hx, hc, w_in, conv_w, conv_b, dt_bias, a_log, d_skip, norm_g, w_out, need_ctx):
    h0 = jnp.zeros((hx.shape[0], SSM_HEADS, SSM_HEADDIM, D_STATE), jnp.float32)
    yc, s_f, s_b = ssd_branch(hc @ w_in, conv_w, conv_b, dt_bias, a_log, d_skip, norm_g, h0, h0)
    yx, _, _ = ssd_branch(hx @ w_in, conv_w, conv_b, dt_bias, a_log, d_skip, norm_g, s_f, s_b)
    out_x = yx @ w_out
    if not need_ctx:
        return out_x, None
    return out_x, yc @ w_out


def setup_inputs(seed: int = 0) -> dict:
    key = jax.random.key(seed)
    ks = iter(list(jax.random.split(key, 40)))
    f32 = jnp.float32
    D = D_MODEL

    def nrm(shape, scale=1.0):
        return jax.random.normal(next(ks), shape, f32) * scale

    def gain(shape):
        return 1.0 + 0.1 * jax.random.normal(next(ks), shape, f32)

    x = nrm((BATCH, SEQ, D))
    c = nrm((BATCH, D))
    ctx = nrm((BATCH, CTX_LEN, D))
    c_ctx = nrm((D,))
    ada_w = nrm((DEPTH, D, 6 * D), 0.5 * D ** -0.5)
    ada_b = nrm((DEPTH, 6 * D), 0.01)
    norm1_g = gain((DEPTH, D))
    norm2_g = gain((DEPTH, D))
    a_w_in = nrm((N_EVEN, D, IN_A), D ** -0.5)
    a_g_q = gain((N_EVEN, Q_LORA))
    a_w_uq = nrm((N_EVEN, Q_LORA, MLA_HEADS * (QK_NOPE + QK_ROPE)), Q_LORA ** -0.5)
    a_g_kv = gain((N_EVEN, KV_LORA))
    a_w_ukv = nrm((N_EVEN, KV_LORA, MLA_HEADS * (QK_NOPE + V_HEAD)), KV_LORA ** -0.5)
    a_conv_w = nrm((N_EVEN, CONV_K, CONV_CH), CONV_K ** -0.5)
    a_conv_b = nrm((N_EVEN, CONV_CH), 0.01)
    a_ln_g = gain((N_EVEN, CONV_CH))
    a_ln_b = nrm((N_EVEN, CONV_CH), 0.01)
    a_w_o = nrm((N_EVEN, MIX_A, D), MIX_A ** -0.5)
    m_w_in = nrm((N_ODD, D, IN_C), D ** -0.5)
    m_conv_w = nrm((N_ODD, SSM_CONV_K, XBC), SSM_CONV_K ** -0.5)
    m_conv_b = nrm((N_ODD, XBC), 0.01)
    dt0 = jnp.exp(jax.random.uniform(next(ks), (N_ODD, 2, SSM_HEADS), f32, math.log(1e-3), math.log(1e-1)))
    m_dt_bias = dt0 + jnp.log(-jnp.expm1(-dt0))
    m_a_log = jnp.log(jax.random.uniform(next(ks), (N_ODD, 2, SSM_HEADS), f32, 1.0, 16.0))
    m_d = gain((N_ODD, SSM_HEADS))
    m_norm_g = gain((N_ODD, D_INNER))
    m_w_out = nrm((N_ODD, D_INNER, D), D_INNER ** -0.5)
    ffn_w1 = nrm((DEPTH, D, D_FF), D ** -0.5)
    ffn_w3 = nrm((DEPTH, D, D_FF), D ** -0.5)
    ffn_w2 = nrm((DEPTH, D_FF, D), D_FF ** -0.5)
    final_g = gain((D,))
    return {'x': x, 'c': c, 'ctx': ctx, 'c_ctx': c_ctx, 'ada_w': ada_w, 'ada_b': ada_b,
            'norm1_g': norm1_g, 'norm2_g': norm2_g, 'a_w_in': a_w_in, 'a_g_q': a_g_q, 'a_w_uq': a_w_uq,
            'a_g_kv': a_g_kv, 'a_w_ukv': a_w_ukv, 'a_conv_w': a_conv_w, 'a_conv_b': a_conv_b,
            'a_ln_g': a_ln_g, 'a_ln_b': a_ln_b, 'a_w_o': a_w_o, 'm_w_in': m_w_in, 'm_conv_w': m_conv_w,
            'm_conv_b': m_conv_b, 'm_dt_bias': m_dt_bias, 'm_a_log': m_a_log, 'm_d': m_d,
            'm_norm_g': m_norm_g, 'm_w_out': m_w_out, 'ffn_w1': ffn_w1, 'ffn_w3': ffn_w3,
            'ffn_w2': ffn_w2, 'final_g': final_g}


def reference(x, c, ctx, c_ctx, ada_w, ada_b, norm1_g, norm2_g, a_w_in, a_g_q, a_w_uq, a_g_kv, a_w_ukv,
              a_conv_w, a_conv_b, a_ln_g, a_ln_b, a_w_o, m_w_in, m_conv_w, m_conv_b, m_dt_bias, m_a_log,
              m_d, m_norm_g, m_w_out, ffn_w1, ffn_w3, ffn_w2, final_g):
    rows = x.shape[1] // GRID_W
    cos, sin = axial_rope_tables(rows)
    for layer in range(DEPTH):
        need_ctx = layer < DEPTH - 1
        sm_x, cm_x, gm_x, sf_x, cf_x, gf_x = adaln(c, ada_w[layer], ada_b[layer])
        sm_c, cm_c, gm_c, sf_c, cf_c, gf_c = adaln(c_ctx, ada_w[layer], ada_b[layer])
        hx = modulate(x, norm1_g[layer], sm_x, cm_x)
        hc = modulate(ctx, norm1_g[layer], sm_c, cm_c)
        if layer % 2 == 0:
            i = layer // 2
            mx, mc = attn_conv_mixer(hx, hc, a_w_in[i], a_g_q[i], a_w_uq[i], a_g_kv[i], a_w_ukv[i],
                                     a_conv_w[i], a_conv_b[i], a_ln_g[i], a_ln_b[i], a_w_o[i], cos, sin, need_ctx)
        else:
            j = layer // 2
            mx, mc = ssd_mixer(hx, hc, m_w_in[j], m_conv_w[j], m_conv_b[j], m_dt_bias[j], m_a_log[j],
                               m_d[j], m_norm_g[j], m_w_out[j], need_ctx)
        x = x + gm_x * mx
        x = x + gf_x * swiglu(modulate(x, norm2_g[layer], sf_x, cf_x), ffn_w1[layer], ffn_w3[layer], ffn_w2[layer])
        if need_ctx:
            ctx = ctx + gm_c * mc
            ctx = ctx + gf_c * swiglu(modulate(ctx, norm2_g[layer], sf_c, cf_c), ffn_w1[layer], ffn_w3[layer], ffn_w2[layer])
    return rmsnorm(x, final_g)
```

```python
import functools
import math

import jax
import jax.numpy as jnp
from jax import lax
from jax.experimental import pallas as pl
from jax.experimental.pallas import tpu as pltpu

F32 = jnp.float32
BF16 = jnp.bfloat16
EPS = 1e-6

GRID_W = 64
ROPE_THETA = 10000.0
QK_NOPE = 128
QK_ROPE = 64
V_HEAD = 128
QK_HEAD = QK_NOPE + 2 * QK_ROPE
ATTN_SCALE = 1.0 / math.sqrt(QK_NOPE + QK_ROPE)
SSM_GROUPS = 8
SSM_HEADDIM = 64
D_STATE = 128
CHUNK = 128
CONV_HALO = 16

V7X_VMEM_BYTES = 64 * 1024 * 1024
VMEM_CAP = V7X_VMEM_BYTES - 6 * 1024 * 1024


def _pick(n, candidates):
    for c in candidates:
        if c <= n and n % c == 0:
            return c
    return n


def _params(semantics, vmem_bytes):
    limit = int(min(max(vmem_bytes * 1.25 + (4 << 20), 24 << 20), VMEM_CAP))
    return pltpu.CompilerParams(dimension_semantics=semantics, vmem_limit_bytes=limit)


def _silu(v):
    return v * jax.nn.sigmoid(v)


def _ctx_rows(tile, bm, n_x):
    rows = tile * bm + lax.broadcasted_iota(jnp.int32, (bm, 1), 0)
    return rows >= n_x


def _adaln_body(c_ref, w_ref, b_ref, o_ref):
    a = _silu(c_ref[...]).astype(BF16)
    o_ref[...] = jnp.dot(a, w_ref[...].astype(BF16), preferred_element_type=F32) + b_ref[...]


def adaln(cond8, ada_w, ada_b):
    depth, d, n = ada_w.shape
    tn = _pick(n, (512, 256, 128))
    return pl.pallas_call(
        _adaln_body,
        out_shape=jax.ShapeDtypeStruct((depth, 8, n), F32),
        grid=(depth, n // tn),
        in_specs=[pl.BlockSpec((8, d), lambda l, j: (0, 0)),
                  pl.BlockSpec((None, d, tn), lambda l, j: (l, 0, j)),
                  pl.BlockSpec((None, 1, tn), lambda l, j: (l, 0, j))],
        out_specs=pl.BlockSpec((None, 8, tn), lambda l, j: (l, 0, j)),
        compiler_params=_params(("arbitrary", "arbitrary"), 3 * d * tn * 4),
        name="adaln",
    )(cond8, ada_w, ada_b.reshape(depth, 1, n))


def _norm_mod_body(x_ref, g_ref, sh_ref, sc_ref, o_ref, *, bm, n_x):
    x = x_ref[...]
    y = x * lax.rsqrt(jnp.mean(x * x, axis=-1, keepdims=True) + EPS) * g_ref[...]
    is_c = _ctx_rows(pl.program_id(0), bm, n_x)
    sc = jnp.where(is_c, sc_ref[1:2, :], sc_ref[0:1, :])
    sh = jnp.where(is_c, sh_ref[1:2, :], sh_ref[0:1, :])
    o_ref[...] = (y * (1.0 + sc) + sh).astype(o_ref.dtype)


def norm_mod(x, g, mods, shift_blk, scale_blk, n_x):
    m, d = x.shape
    bm = _pick(m, (256, 128))
    return pl.pallas_call(
        functools.partial(_norm_mod_body, bm=bm, n_x=n_x),
        out_shape=jax.ShapeDtypeStruct((m, d), BF16),
        grid=(m // bm,),
        in_specs=[pl.BlockSpec((bm, d), lambda i: (i, 0)),
                  pl.BlockSpec((1, d), lambda i: (0, 0)),
                  pl.BlockSpec((8, d), lambda i: (0, shift_blk)),
                  pl.BlockSpec((8, d), lambda i: (0, scale_blk))],
        out_specs=pl.BlockSpec((bm, d), lambda i: (i, 0)),
        compiler_params=_params(("arbitrary",), 2 * bm * d * 6 + 6 * bm * d * 4),
        name="norm_mod",
    )(x, g.reshape(1, d), mods, mods)


def _rmsnorm_body(x_ref, g_ref, o_ref):
    x = x_ref[...]
    o_ref[...] = x * lax.rsqrt(jnp.mean(x * x, axis=-1, keepdims=True) + EPS) * g_ref[...]


def final_rmsnorm(x, g):
    m, d = x.shape
    bm = _pick(m, (256, 128))
    return pl.pallas_call(
        _rmsnorm_body,
        out_shape=jax.ShapeDtypeStruct((m, d), F32),
        grid=(m // bm,),
        in_specs=[pl.BlockSpec((bm, d), lambda i: (i, 0)), pl.BlockSpec((1, d), lambda i: (0, 0))],
        out_specs=pl.BlockSpec((bm, d), lambda i: (i, 0)),
        compiler_params=_params(("arbitrary",), 6 * bm * d * 4),
        name="final_rmsnorm",
    )(x, g.reshape(1, d))


def _linear_body(*refs, n_pairs, swiglu, has_res, bm, n_x):
    a_refs = refs[:n_pairs]
    w_refs = refs[n_pairs:2 * n_pairs]
    pos = 2 * n_pairs
    if swiglu:
        w3_refs = refs[pos:pos + n_pairs]
        pos += n_pairs
    if has_res:
        res_ref, gate_ref = refs[pos], refs[pos + 1]
        pos += 2
    o_ref = refs[pos]

    def contract(ws):
        acc = None
        for a_ref, w_ref in zip(a_refs, ws):
            part = jnp.dot(a_ref[...], w_ref[...], preferred_element_type=F32)
            acc = part if acc is None else acc + part
        return acc

    acc = contract(w_refs)
    if swiglu:
        acc = _silu(acc) * contract(w3_refs)
    if has_res:
        is_c = _ctx_rows(pl.program_id(0), bm, n_x)
        gate = jnp.where(is_c, gate_ref[1:2, :], gate_ref[0:1, :])
        acc = res_ref[...] + gate * acc
    o_ref[...] = acc.astype(o_ref.dtype)


def linear(a_list, w_list, *, out_dtype, w3_list=None, res=None, mods=None, gate_blk=None, n_x=None,
           bm_opts=(1280, 1024, 640, 512, 256, 128), bn_opts=(512, 640, 256, 128), name="linear"):
    m = a_list[0].shape[0]
    n = w_list[0].shape[1]
    bm = _pick(m, bm_opts)
    bn = _pick(n, bn_opts)
    swiglu = w3_list is not None
    has_res = res is not None
    n_pairs = len(a_list)
    in_specs, args, vmem = [], [], 0
    for a in a_list:
        k = a.shape[1]
        in_specs.append(pl.BlockSpec((bm, k), lambda i, j: (i, 0)))
        args.append(a)
        vmem += 2 * bm * k * a.dtype.itemsize
    for ws in ([w_list, w3_list] if swiglu else [w_list]):
        for w in ws:
            k = w.shape[0]
            in_specs.append(pl.BlockSpec((k, bn), lambda i, j: (0, j)))
            args.append(w)
            vmem += 2 * k * bn * w.dtype.itemsize
    if has_res:
        in_specs.append(pl.BlockSpec((bm, bn), lambda i, j: (i, j)))
        args.append(res)
        gb = gate_blk * (n // bn)
        in_specs.append(pl.BlockSpec((8, bn), lambda i, j: (0, gb + j)))
        args.append(mods)
        vmem += 2 * bm * bn * 4
    vmem += 2 * bm * bn * jnp.dtype(out_dtype).itemsize + 3 * bm * bn * 4
    return pl.pallas_call(
        functools.partial(_linear_body, n_pairs=n_pairs, swiglu=swiglu, has_res=has_res, bm=bm,
                          n_x=m if n_x is None else n_x),
        out_shape=jax.ShapeDtypeStruct((m, n), out_dtype),
        grid=(m // bm, n // bn),
        in_specs=in_specs,
        out_specs=pl.BlockSpec((bm, bn), lambda i, j: (i, j)),
        compiler_params=_params(("arbitrary", "arbitrary"), vmem),
        name=name,
    )(*args)


def _qkv_body(cq_ref, ckv_ref, kr_ref, tq_ref, tk_ref, gq_ref, gkv_ref, wq_ref, wkv_ref,
              q_ref, k_ref, v_ref, *, heads):
    def rms(v, g):
        return (v * lax.rsqrt(jnp.mean(v * v, axis=-1, keepdims=True) + EPS) * g).astype(BF16)

    nq = rms(cq_ref[...], gq_ref[...])
    nkv = rms(ckv_ref[...], gkv_ref[...])
    tq = tq_ref[...]
    kp = kr_ref[...] * tk_ref[...]
    kr2 = (kp + pltpu.roll(kp, QK_ROPE, 1)).astype(BF16)
    for h in range(heads):
        lo = h * QK_HEAD
        t = jnp.dot(nq, wq_ref[:, lo:lo + QK_HEAD], preferred_element_type=F32)
        q_ref[:, lo:lo + QK_NOPE] = (t[:, :QK_NOPE] * ATTN_SCALE).astype(BF16)
        q_ref[:, lo + QK_NOPE:lo + QK_HEAD] = (t[:, QK_NOPE:] * tq).astype(BF16)
        t = jnp.dot(nkv, wkv_ref[:, lo:lo + QK_HEAD], preferred_element_type=F32)
        k_ref[:, lo:lo + QK_NOPE] = t[:, :QK_NOPE].astype(BF16)
        k_ref[:, lo + QK_NOPE:lo + QK_HEAD] = kr2
        v_ref[:, h * V_HEAD:(h + 1) * V_HEAD] = t[:, QK_NOPE:].astype(BF16)


def mla_qkv(u, off_cq, q_lora, off_ckv, kv_lora, off_kr, tq, tk, g_q, g_kv, wq, wkv, heads):
    m = u.shape[0]
    bm = _pick(m, (256, 128))
    rows = lambda i: (i, 0)
    fixed = lambda i: (0, 0)
    hw = heads * QK_HEAD
    vmem = 2 * bm * (q_lora + kv_lora + 3 * 128) * 4 + 2 * (q_lora + kv_lora) * hw * 2 \
        + 2 * bm * (2 * hw + heads * V_HEAD) * 2 + 4 * bm * hw * 4
    return pl.pallas_call(
        functools.partial(_qkv_body, heads=heads),
        out_shape=(jax.ShapeDtypeStruct((m, hw), BF16), jax.ShapeDtypeStruct((m, hw), BF16),
                   jax.ShapeDtypeStruct((m, heads * V_HEAD), BF16)),
        grid=(m // bm,),
        in_specs=[pl.BlockSpec((bm, q_lora), lambda i: (i, off_cq // q_lora)),
                  pl.BlockSpec((bm, kv_lora), lambda i: (i, off_ckv // kv_lora)),
                  pl.BlockSpec((bm, 128), lambda i: (i, off_kr // 128)),
                  pl.BlockSpec((bm, 128), rows), pl.BlockSpec((bm, 128), rows),
                  pl.BlockSpec((1, q_lora), fixed), pl.BlockSpec((1, kv_lora), fixed),
                  pl.BlockSpec((q_lora, hw), fixed), pl.BlockSpec((kv_lora, hw), fixed)],
        out_specs=(pl.BlockSpec((bm, hw), rows), pl.BlockSpec((bm, hw), rows),
                   pl.BlockSpec((bm, heads * V_HEAD), rows)),
        compiler_params=_params(("arbitrary",), vmem),
        name="mla_qkv",
    )(u, u, u, tq, tk, g_q.reshape(1, -1), g_kv.reshape(1, -1), wq, wkv)


def _flash_body(*refs, bk, n_blocks, aliased):
    q_ref, k_ref, v_ref = refs[:3]
    o_ref = refs[4] if aliased else refs[3]
    q = q_ref[...]
    bq = q.shape[0]

    def step(t, carry):
        m_i, l_i, acc = carry
        start = pl.multiple_of(t * bk, bk)
        s = lax.dot_general(q, k_ref[pl.ds(start, bk), :], (((1,), (1,)), ((), ())),
                            preferred_element_type=F32)
        m_new = jnp.maximum(m_i, jnp.max(s, axis=-1, keepdims=True))
        alpha = jnp.exp(m_i - m_new)
        p = jnp.exp(s - m_new)
        l_new = alpha * l_i + jnp.sum(p, axis=-1, keepdims=True)
        acc = alpha * acc + jnp.dot(p.astype(BF16), v_ref[pl.ds(start, bk), :], preferred_element_type=F32)
        return m_new, l_new, acc

    init = (jnp.full((bq, 1), -jnp.inf, F32), jnp.zeros((bq, 1), F32), jnp.zeros((bq, V_HEAD), F32))
    _, l_i, acc = lax.fori_loop(0, n_blocks, step, init)
    o_ref[...] = (acc / l_i).astype(o_ref.dtype)


def attention(q, k, v, heads, *, q_rows, kv_row0, kv_rows, out=None):
    m = q.shape[0]
    aliased = out is not None
    q_row0 = kv_row0 if aliased else 0
    bq = _pick(q_rows, (512, 256, 128))
    bk = _pick(kv_rows, (1280, 1024, 512, 256, 128))
    assert q_row0 % bq == 0 and kv_row0 % kv_rows == 0
    qb0, kb0 = q_row0 // bq, kv_row0 // kv_rows
    in_specs = [pl.BlockSpec((bq, QK_HEAD), lambda h, i: (qb0 + i, h)),
                pl.BlockSpec((kv_rows, QK_HEAD), lambda h, i: (kb0, h)),
                pl.BlockSpec((kv_rows, V_HEAD), lambda h, i: (kb0, h))]
    args = [q, k, v]
    if aliased:
        in_specs.append(pl.BlockSpec(memory_space=pl.ANY))
        args.append(out)
    vmem = 2 * kv_rows * (QK_HEAD + V_HEAD) * 2 + 4 * bq * QK_HEAD * 2 + 4 * bq * bk * 4
    return pl.pallas_call(
        functools.partial(_flash_body, bk=bk, n_blocks=kv_rows // bk, aliased=aliased),
        out_shape=jax.ShapeDtypeStruct((m, heads * V_HEAD), BF16),
        grid=(heads, q_rows // bq),
        in_specs=in_specs,
        out_specs=pl.BlockSpec((bq, V_HEAD), lambda h, i: (qb0 + i, h)),
        input_output_aliases={3: 0} if aliased else {},
        compiler_params=_params(("arbitrary", "arbitrary"), vmem),
        name="attention_ctx" if aliased else "attention",
    )(*args)


def _seq_edges(tile, tiles_x, tiles_all):
    first = jnp.logical_or(tile == 0, tile == tiles_x)
    last = jnp.logical_or(tile == tiles_x - 1, tile == tiles_all - 1)
    return first, last


def _depthwise(ext_ref, w_ref, bm, taps, width, rc=64, cc=512):
    base = CONV_HALO - taps // 2
    rc = min(rc, bm)
    cc = min(cc, width)
    for r0 in range(0, bm, rc):
        for c0 in range(0, width, cc):
            acc = jnp.zeros((rc, cc), F32)
            for k in range(taps):
                acc = acc + ext_ref[r0 + base + k:r0 + base + k + rc, c0:c0 + cc] * w_ref[k:k + 1, c0:c0 + cc]
            yield r0, rc, c0, cc, acc


def _conformer_body(a_ref, b_ref, ap_ref, bp_ref, an_ref, bn_ref, w_ref, cb_ref, g_ref, be_ref, o_ref,
                    ext_ref, cv_ref, *, bm, taps, tiles_x, tiles_all):
    first, last = _seq_edges(pl.program_id(0), tiles_x, tiles_all)
    width = a_ref.shape[1]
    glu = lambda a, b: a * jax.nn.sigmoid(b)
    ext_ref[CONV_HALO:CONV_HALO + bm, :] = glu(a_ref[...], b_ref[...])
    ext_ref[0:CONV_HALO, :] = jnp.where(first, 0.0, glu(ap_ref[...], bp_ref[...]))
    ext_ref[CONV_HALO + bm:, :] = jnp.where(last, 0.0, glu(an_ref[...], bn_ref[...]))
    for r0, rc, c0, cc, acc in _depthwise(ext_ref, w_ref, bm, taps, width):
        cv_ref[r0:r0 + rc, c0:c0 + cc] = acc + cb_ref[:, c0:c0 + cc]
    y = cv_ref[...]
    mu = jnp.mean(y, axis=-1, keepdims=True)
    yc = y - mu
    yn = yc * lax.rsqrt(jnp.mean(yc * yc, axis=-1, keepdims=True) + EPS) * g_ref[...] + be_ref[...]
    o_ref[...] = _silu(yn).astype(o_ref.dtype)


def conformer_branch(u, ch, conv_w, conv_b, ln_g, ln_b, n_x):
    m = u.shape[0]
    taps = conv_w.shape[0]
    bm = _pick(math.gcd(n_x, m - n_x), (256, 128))
    hb = bm // CONV_HALO
    n_hb = m // CONV_HALO
    tiles_all = m // bm
    prev = lambda i: (jnp.maximum(i * hb - 1, 0), 0)
    prev_b = lambda i: (jnp.maximum(i * hb - 1, 0), 1)
    nxt = lambda i: (jnp.minimum((i + 1) * hb, n_hb - 1), 0)
    nxt_b = lambda i: (jnp.minimum((i + 1) * hb, n_hb - 1), 1)
    row = lambda i: (0, 0)
    vmem = 2 * 2 * (bm + 2 * CONV_HALO) * ch * 4 + 2 * bm * ch * 2 + (2 * bm + 2 * CONV_HALO) * ch * 4 \
        + 4 * bm * ch * 4
    return pl.pallas_call(
        functools.partial(_conformer_body, bm=bm, taps=taps, tiles_x=n_x // bm, tiles_all=tiles_all),
        out_shape=jax.ShapeDtypeStruct((m, ch), BF16),
        grid=(tiles_all,),
        in_specs=[pl.BlockSpec((bm, ch), lambda i: (i, 0)), pl.BlockSpec((bm, ch), lambda i: (i, 1)),
                  pl.BlockSpec((CONV_HALO, ch), prev), pl.BlockSpec((CONV_HALO, ch), prev_b),
                  pl.BlockSpec((CONV_HALO, ch), nxt), pl.BlockSpec((CONV_HALO, ch), nxt_b),
                  pl.BlockSpec((taps, ch), row), pl.BlockSpec((1, ch), row),
                  pl.BlockSpec((1, ch), row), pl.BlockSpec((1, ch), row)],
        out_specs=pl.BlockSpec((bm, ch), lambda i: (i, 0)),
        scratch_shapes=[pltpu.VMEM((bm + 2 * CONV_HALO, ch), F32), pltpu.VMEM((bm, ch), F32)],
        compiler_params=_params(("arbitrary",), vmem),
        name="conformer_branch",
    )(u, u, u, u, u, u, conv_w, conv_b.reshape(1, ch), ln_g.reshape(1, ch), ln_b.reshape(1, ch))


def _ssd_conv_body(x_ref, xp_ref, xn_ref, w_ref, cb_ref, o_ref, ext_ref, *, bm, taps, tiles_x, tiles_all):
    first, last = _seq_edges(pl.program_id(1), tiles_x, tiles_all)
    width = x_ref.shape[1]
    ext_ref[CONV_HALO:CONV_HALO + bm, :] = x_ref[...]
    ext_ref[0:CONV_HALO, :] = jnp.where(first, 0.0, xp_ref[...])
    ext_ref[CONV_HALO + bm:, :] = jnp.where(last, 0.0, xn_ref[...])
    for r0, rc, c0, cc, acc in _depthwise(ext_ref, w_ref, bm, taps, width):
        o_ref[r0:r0 + rc, c0:c0 + cc] = _silu(acc + cb_ref[:, c0:c0 + cc])


def ssd_conv(xbc, conv_w, conv_b, n_x):
    m, ch = xbc.shape
    taps = conv_w.shape[0]
    bm = _pick(math.gcd(n_x, m - n_x), (256, 128))
    bc = _pick(ch, (1024, 512, 256, 128))
    hb = bm // CONV_HALO
    n_hb = m // CONV_HALO
    tiles_all = m // bm
    vmem = 2 * (2 * bm + 2 * CONV_HALO) * bc * 4 + (bm + 2 * CONV_HALO) * bc * 4 + 2 * bm * bc * 4
    return pl.pallas_call(
        functools.partial(_ssd_conv_body, bm=bm, taps=taps, tiles_x=n_x // bm, tiles_all=tiles_all),
        out_shape=jax.ShapeDtypeStruct((m, ch), F32),
        grid=(ch // bc, tiles_all),
        in_specs=[pl.BlockSpec((bm, bc), lambda c, i: (i, c)),
                  pl.BlockSpec((CONV_HALO, bc), lambda c, i: (jnp.maximum(i * hb - 1, 0), c)),
                  pl.BlockSpec((CONV_HALO, bc), lambda c, i: (jnp.minimum((i + 1) * hb, n_hb - 1), c)),
                  pl.BlockSpec((taps, bc), lambda c, i: (0, c)),
                  pl.BlockSpec((1, bc), lambda c, i: (0, c))],
        out_specs=pl.BlockSpec((bm, bc), lambda c, i: (i, c)),
        scratch_shapes=[pltpu.VMEM((bm + 2 * CONV_HALO, bc), F32)],
        compiler_params=_params(("arbitrary", "arbitrary"), vmem),
        name="ssd_conv",
    )(xbc, xbc, xbc, conv_w, conv_b.reshape(1, ch))


def _split3(v):
    hi = v.astype(BF16)
    r1 = v - hi.astype(F32)
    mid = r1.astype(BF16)
    lo = (r1 - mid.astype(F32)).astype(BF16)
    return hi, mid, lo


def _dt_body(raw_ref, bias_ref, alog_ref, dt_ref, ac_ref, *, n_heads):
    v = raw_ref[...] + bias_ref[...]
    dt = jnp.maximum(v, 0.0) + jnp.log1p(jnp.exp(-jnp.abs(v)))
    dt_ref[...] = dt
    dta = dt * (-jnp.exp(alog_ref[...]))
    ii = lax.broadcasted_iota(jnp.int32, (CHUNK, CHUNK), 0)
    jj = lax.broadcasted_iota(jnp.int32, (CHUNK, CHUNK), 1)
    lower = jnp.where(ii >= jj, 1.0, 0.0).astype(BF16)
    upper = jnp.where(ii <= jj, 1.0, 0.0).astype(BF16)
    pre = jnp.zeros(dta.shape, F32)
    suf = jnp.zeros(dta.shape, F32)
    for part in _split3(dta)[::-1]:
        pre = pre + jnp.dot(lower, part, preferred_element_type=F32)
        suf = suf + jnp.dot(upper, part, preferred_element_type=F32)
    col = lax.broadcasted_iota(jnp.int32, dta.shape, 1)
    ac_ref[...] = jnp.where(col < n_heads, pre, suf)


def ssd_dt(raw, dt_bias, a_log):
    m, w = raw.shape
    blk = pl.BlockSpec((CHUNK, w), lambda i: (i, 0))
    vec = pl.BlockSpec((1, w), lambda i: (0, 0))
    return pl.pallas_call(
        functools.partial(_dt_body, n_heads=w // 2),
        out_shape=(jax.ShapeDtypeStruct((m, w), F32), jax.ShapeDtypeStruct((m, w), F32)),
        grid=(m // CHUNK,),
        in_specs=[blk, vec, vec],
        out_specs=(blk, blk),
        compiler_params=_params(("arbitrary",), 16 * CHUNK * w * 4),
        name="ssd_dt",
    )(raw, dt_bias.reshape(1, w), a_log.reshape(1, w))


def _ssd_body(x_ref, b_ref, c_ref, dt_ref, ac_ref, ar_ref, y_ref, st_ref, *, hg, p):
    d = pl.program_id(0)

    @pl.when(pl.program_id(2) == 0)
    def _():
        st_ref[...] = jnp.zeros(st_ref.shape, F32)

    bmat = b_ref[...].astype(BF16)
    cmat = c_ref[...].astype(BF16)
    cb = lax.dot_general(cmat, bmat, (((1,), (1,)), ((), ())), preferred_element_type=F32)
    ii = lax.broadcasted_iota(jnp.int32, (CHUNK, CHUNK), 0)
    jj = lax.broadcasted_iota(jnp.int32, (CHUNK, CHUNK), 1)
    mask = jnp.where(d == 0, ii - jj, jj - ii) >= 0
    x = x_ref[...]
    dt = dt_ref[...]
    ac = ac_ref[...]
    ar = ar_ref[...]
    st = st_ref[...]
    y_state = jnp.dot(cmat, st.astype(BF16), preferred_element_type=F32)
    last = jnp.where(d == 0, ac[CHUNK - 1:CHUNK, :], ac[0:1, :])
    for h in range(hg):
        cols = slice(h * p, (h + 1) * p)
        a_col = ac[:, h:h + 1]
        dt_col = dt[:, h:h + 1]
        a_last = last[:, h:h + 1]
        decay = jnp.exp(jnp.where(mask, a_col - ar[h:h + 1, :], -jnp.inf))
        xh = x[:, cols]
        y_in = jnp.dot((cb * decay).astype(BF16), (dt_col * xh).astype(BF16), preferred_element_type=F32)
        y_ref[:, cols] = y_in + y_state[:, cols] * jnp.exp(a_col)
        wx = (jnp.exp(a_last - a_col) * dt_col * xh).astype(BF16)
        upd = lax.dot_general(bmat, wx, (((0,), (0,)), ((), ())), preferred_element_type=F32)
        st_ref[:, cols] = st[:, cols] * jnp.exp(a_last) + upd


def ssd_scan(xbc_c, dt_col, ac_col, ac_row, d_inner, n_x):
    m = xbc_c.shape[0]
    groups = SSM_GROUPS
    gw = d_inner // groups
    hg = gw // SSM_HEADDIM
    n_xc = n_x // CHUNK
    n_cc = m // CHUNK - n_xc
    b_blk0 = d_inner // D_STATE
    c_blk0 = b_blk0 + groups

    def chunk(d, t):
        fwd = jnp.where(t < n_cc, n_xc + t, t - n_cc)
        bwd = jnp.where(t < n_cc, n_xc + n_cc - 1 - t, n_xc + n_cc - 1 - t)
        return jnp.where(d == 0, fwd, bwd)

    small = pl.BlockSpec((None, None, CHUNK, hg), lambda d, g, t: (d, g, chunk(d, t), 0))
    vmem = 2 * CHUNK * gw * 4 * 2 + 4 * CHUNK * D_STATE * 4 + 8 * CHUNK * 128 * 4 + D_STATE * gw * 4 \
        + 6 * CHUNK * gw * 4
    return pl.pallas_call(
        functools.partial(_ssd_body, hg=hg, p=SSM_HEADDIM),
        out_shape=jax.ShapeDtypeStruct((2, m, d_inner), F32),
        grid=(2, groups, m // CHUNK),
        in_specs=[pl.BlockSpec((CHUNK, gw), lambda d, g, t: (chunk(d, t), g)),
                  pl.BlockSpec((CHUNK, D_STATE), lambda d, g, t: (chunk(d, t), b_blk0 + g)),
                  pl.BlockSpec((CHUNK, D_STATE), lambda d, g, t: (chunk(d, t), c_blk0 + g)),
                  small, small,
                  pl.BlockSpec((None, None, None, hg, CHUNK), lambda d, g, t: (d, g, chunk(d, t), 0, 0))],
        out_specs=pl.BlockSpec((None, CHUNK, gw), lambda d, g, t: (d, chunk(d, t), g)),
        scratch_shapes=[pltpu.VMEM((D_STATE, gw), F32)],
        compiler_params=_params(("arbitrary", "arbitrary", "arbitrary"), vmem),
        name="ssd_scan",
    )(xbc_c, xbc_c, xbc_c, dt_col, ac_col, ac_row)


def _gate_norm_body(yf_ref, yb_ref, xs_ref, z_ref, dsk_ref, g_ref, o_ref):
    y = yf_ref[...] + yb_ref[...] + dsk_ref[...] * xs_ref[...]
    y = y * _silu(z_ref[...])
    o_ref[...] = (y * lax.rsqrt(jnp.mean(y * y, axis=-1, keepdims=True) + EPS) * g_ref[...]).astype(o_ref.dtype)


def ssd_gate_norm(y2, xbc_c, z, d_skip_cols, norm_g, n_x):
    di = z.shape[1]
    bm = _pick(n_x, (128,))
    blk = pl.BlockSpec((bm, di), lambda i: (i, 0))
    vec = pl.BlockSpec((1, di), lambda i: (0, 0))
    return pl.pallas_call(
        _gate_norm_body,
        out_shape=jax.ShapeDtypeStruct((n_x, di), BF16),
        grid=(n_x // bm,),
        in_specs=[pl.BlockSpec((None, bm, di), lambda i: (0, i, 0)),
                  pl.BlockSpec((None, bm, di), lambda i: (1, i, 0)), blk, blk, vec, vec],
        out_specs=blk,
        compiler_params=_params(("arbitrary",), 2 * bm * di * 18 + 4 * bm * di * 4),
        name="ssd_gate_norm",
    )(y2, y2, xbc_c, z, d_skip_cols.reshape(1, di), norm_g.reshape(1, di))


def _rope_perm():
    d = jnp.arange(QK_ROPE)
    first = (d % (QK_ROPE // 2)) < QK_ROPE // 4
    return jnp.where(first, d + QK_ROPE // 4, d - QK_ROPE // 4), jnp.where(first, -1.0, 1.0).astype(F32)


def _rope_tables(n_x, n_ctx):
    rows = n_x // GRID_W
    row = jnp.repeat(jnp.arange(rows, dtype=F32), GRID_W)
    col = jnp.tile(jnp.arange(GRID_W, dtype=F32), rows)
    quarter = QK_ROPE // 4
    freqs = 1.0 / (ROPE_THETA ** (jnp.arange(quarter, dtype=F32) / quarter))
    ar = row[:, None] * freqs
    ac = col[:, None] * freqs
    ang = jnp.concatenate([ar, ar, ac, ac], axis=-1)
    tab = jnp.concatenate([jnp.cos(ang), jnp.sin(ang)], axis=-1)
    ctx_tab = jnp.concatenate([jnp.ones((n_ctx, QK_ROPE), F32), jnp.zeros((n_ctx, QK_ROPE), F32)], axis=-1)
    return jnp.concatenate([tab, ctx_tab], axis=0)


def _ffn(xa, mods, norm_g, w1, w3, w2, n_x):
    h = norm_mod(xa, norm_g, mods, 3, 4, n_x)
    g = linear([h], [w1.astype(BF16)], w3_list=[w3.astype(BF16)], out_dtype=BF16, bn_opts=(256, 128),
               name="ffn_up")
    return linear([g], [w2.astype(BF16)], out_dtype=F32, res=xa, mods=mods, gate_blk=5, n_x=n_x,
                  bm_opts=(640, 512, 256, 128), bn_opts=(256, 128), name="ffn_down")


def _attn_conv_layer(xa, mods, n_x, norm1_g, w_in, g_q, w_uq, g_kv, w_ukv, conv_w, conv_b, ln_g, ln_b, w_o):
    m = xa.shape[0]
    q_lora, kv_lora, ch = g_q.shape[0], g_kv.shape[0], conv_b.shape[0]
    heads = w_uq.shape[1] // (QK_NOPE + QK_ROPE)
    perm, sign = _rope_perm()
    c0 = q_lora + kv_lora + QK_ROPE
    w_kr = w_in[:, q_lora + kv_lora:c0]
    w_in_r = jnp.concatenate([w_in[:, c0:], w_in[:, :q_lora + kv_lora], w_kr, w_kr[:, perm] * sign],
                             axis=1).astype(BF16)
    off_cq, off_ckv, off_kr = 2 * ch, 2 * ch + q_lora, 2 * ch + q_lora + kv_lora
    wq3 = w_uq.reshape(q_lora, heads, QK_NOPE + QK_ROPE)
    wq_rope = wq3[:, :, QK_NOPE:]
    wq = jnp.concatenate([wq3, wq_rope[:, :, perm] * sign], axis=-1).reshape(q_lora, heads * QK_HEAD).astype(BF16)
    wkv = w_ukv.astype(BF16)
    tk = _rope_tables(n_x, m - n_x)
    tq = tk * ATTN_SCALE

    h = norm_mod(xa, norm1_g, mods, 0, 1, n_x)
    u = linear([h], [w_in_r], out_dtype=F32, bn_opts=(640, 512, 256, 128), name="attn_in_proj")
    q, k, v = mla_qkv(u, off_cq, q_lora, off_ckv, kv_lora, off_kr, tq, tk, g_q, g_kv, wq, wkv, heads)
    cv = conformer_branch(u, ch, conv_w, conv_b, ln_g, ln_b, n_x)
    o = attention(q, k, v, heads, q_rows=n_x, kv_row0=0, kv_rows=m)
    o = attention(q, k, v, heads, q_rows=m - n_x, kv_row0=n_x, kv_rows=m - n_x, out=o)
    hv = heads * V_HEAD
    return linear([o, cv], [w_o[:hv].astype(BF16), w_o[hv:].astype(BF16)], out_dtype=F32, res=xa, mods=mods,
                  gate_blk=2, n_x=n_x, name="attn_out_proj")


def _ssd_layer(xa, mods, n_x, norm1_g, w_in, conv_w, conv_b, dt_bias, a_log, d_skip, norm_g, w_out):
    m = xa.shape[0]
    d_inner = norm_g.shape[0]
    n_heads = d_skip.shape[0]
    groups = SSM_GROUPS
    hg = n_heads // groups
    xbc_w = conv_b.shape[0]
    h = norm_mod(xa, norm1_g, mods, 0, 1, n_x)
    z = linear([h], [w_in[:, :d_inner].astype(BF16)], out_dtype=F32, name="ssd_in_z")
    xbc = linear([h], [w_in[:, d_inner:d_inner + xbc_w].astype(BF16)], out_dtype=F32, name="ssd_in_xbc")
    dt_raw = linear([h], [w_in[:, d_inner + xbc_w:].astype(BF16)], out_dtype=F32, name="ssd_in_dt")
    xbc_c = ssd_conv(xbc, conv_w, conv_b, n_x)
    dt, ac = ssd_dt(dt_raw, dt_bias.reshape(-1), a_log.reshape(-1))
    to_col = lambda t: t.reshape(m, 2, groups, hg).transpose(1, 2, 0, 3)
    ac_row = ac.reshape(m // CHUNK, CHUNK, 2, groups, hg).transpose(2, 3, 0, 4, 1)
    y2 = ssd_scan(xbc_c, to_col(dt), to_col(ac), ac_row, d_inner, n_x)
    yn = ssd_gate_norm(y2, xbc_c, z, jnp.repeat(d_skip, SSM_HEADDIM), norm_g, n_x)
    return linear([yn], [w_out.astype(BF16)], out_dtype=F32, res=xa, mods=mods, gate_blk=2,
                  bm_opts=(1024, 512, 256, 128), bn_opts=(256, 128), name="ssd_out_proj")


def kernel(x, c, ctx, c_ctx, ada_w, ada_b, norm1_g, norm2_g, a_w_in, a_g_q, a_w_uq, a_g_kv, a_w_ukv, a_conv_w,
           a_conv_b, a_ln_g, a_ln_b, a_w_o, m_w_in, m_conv_w, m_conv_b, m_dt_bias, m_a_log, m_d, m_norm_g,
           m_w_out, ffn_w1, ffn_w3, ffn_w2, final_g):
    assert x.shape[0] == 1 and ada_w.shape[0] == 2
    n_x, d = x.shape[1], x.shape[2]
    xa = jnp.concatenate([x[0], ctx[0]], axis=0)
    cond8 = jnp.zeros((8, d), F32).at[0].set(c[0]).at[1].set(c_ctx)
    mods = adaln(cond8, ada_w, ada_b)

    xa = _attn_conv_layer(xa, mods[0], n_x, norm1_g[0], a_w_in[0], a_g_q[0], a_w_uq[0], a_g_kv[0], a_w_ukv[0],
                          a_conv_w[0], a_conv_b[0], a_ln_g[0], a_ln_b[0], a_w_o[0])
    xa = _ffn(xa, mods[0], norm2_g[0], ffn_w1[0], ffn_w3[0], ffn_w2[0], n_x)
    xl = _ssd_layer(xa, mods[1], n_x, norm1_g[1], m_w_in[0], m_conv_w[0], m_conv_b[0], m_dt_bias[0], m_a_log[0],
                    m_d[0], m_norm_g[0], m_w_out[0])
    xl = _ffn(xl, mods[1], norm2_g[1], ffn_w1[1], ffn_w3[1], ffn_w2[1], xl.shape[0])
    return final_rmsnorm(xl, final_g)[None]
```

```python
import functools
import math

import jax
import jax.numpy as jnp
from jax import lax
from jax.experimental import pallas as pl
from jax.experimental.pallas import tpu as pltpu

F32 = jnp.float32
BF16 = jnp.bfloat16
EPS = 1e-6

GRID_W = 64
ROPE_THETA = 10000.0
QK_NOPE = 128
QK_ROPE = 64
V_HEAD = 128
QK_HEAD = QK_NOPE + 2 * QK_ROPE
ATTN_SCALE = 1.0 / math.sqrt(QK_NOPE + QK_ROPE)
Q_SCALE = ATTN_SCALE * math.log2(math.e)
SSM_GROUPS = 8
SSM_HEADDIM = 64
D_STATE = 128
CHUNK = 128
CONV_HALO = 16

V7X_VMEM_BYTES = 64 * 1024 * 1024
VMEM_CAP = V7X_VMEM_BYTES - 6 * 1024 * 1024


def _pick(n, candidates):
    for c in candidates:
        if c <= n and n % c == 0:
            return c
    return n


def _params(semantics, vmem_bytes, flags=None):
    limit = int(min(max(vmem_bytes * 1.25 + (4 << 20), 24 << 20), VMEM_CAP))
    return pltpu.CompilerParams(dimension_semantics=semantics, vmem_limit_bytes=limit, flags=flags)


def _silu(v):
    return v * jax.nn.sigmoid(v)


def _ctx_rows(tile, bm, n_x):
    rows = tile * bm + lax.broadcasted_iota(jnp.int32, (bm, 1), 0)
    return rows >= n_x


def _adaln_body(c_ref, w_ref, b_ref, o_ref):
    a = _silu(c_ref[...]).astype(BF16)
    o_ref[...] = jnp.dot(a, w_ref[...].astype(BF16), preferred_element_type=F32) + b_ref[...]


def adaln(cond8, ada_w, ada_b):
    depth, d, n = ada_w.shape
    tn = _pick(n, (512, 256, 128))
    return pl.pallas_call(
        _adaln_body,
        out_shape=jax.ShapeDtypeStruct((depth, 8, n), F32),
        grid=(depth, n // tn),
        in_specs=[pl.BlockSpec((8, d), lambda l, j: (0, 0)),
                  pl.BlockSpec((None, d, tn), lambda l, j: (l, 0, j)),
                  pl.BlockSpec((None, 1, tn), lambda l, j: (l, 0, j))],
        out_specs=pl.BlockSpec((None, 8, tn), lambda l, j: (l, 0, j)),
        compiler_params=_params(("arbitrary", "arbitrary"), 3 * d * tn * 4),
        name="adaln",
    )(cond8, ada_w, ada_b.reshape(depth, 1, n))


def _norm_mod_body(x_ref, g_ref, sh_ref, sc_ref, o_ref, *, bm, n_x):
    x = x_ref[...]
    y = x * lax.rsqrt(jnp.mean(x * x, axis=-1, keepdims=True) + EPS) * g_ref[...]
    is_c = _ctx_rows(pl.program_id(0), bm, n_x)
    sc = jnp.where(is_c, sc_ref[1:2, :], sc_ref[0:1, :])
    sh = jnp.where(is_c, sh_ref[1:2, :], sh_ref[0:1, :])
    o_ref[...] = (y * (1.0 + sc) + sh).astype(o_ref.dtype)


def norm_mod(x, g, mods, shift_blk, scale_blk, n_x):
    m, d = x.shape
    bm = _pick(m, (256, 128))
    return pl.pallas_call(
        functools.partial(_norm_mod_body, bm=bm, n_x=n_x),
        out_shape=jax.ShapeDtypeStruct((m, d), BF16),
        grid=(m // bm,),
        in_specs=[pl.BlockSpec((bm, d), lambda i: (i, 0)),
                  pl.BlockSpec((1, d), lambda i: (0, 0)),
                  pl.BlockSpec((8, d), lambda i: (0, shift_blk)),
                  pl.BlockSpec((8, d), lambda i: (0, scale_blk))],
        out_specs=pl.BlockSpec((bm, d), lambda i: (i, 0)),
        compiler_params=_params(("arbitrary",), 2 * bm * d * 6 + 6 * bm * d * 4),
        name="norm_mod",
    )(x, g.reshape(1, d), mods, mods)


def _rmsnorm_body(x_ref, g_ref, o_ref):
    x = x_ref[...]
    o_ref[...] = x * lax.rsqrt(jnp.mean(x * x, axis=-1, keepdims=True) + EPS) * g_ref[...]


def final_rmsnorm(x, g):
    m, d = x.shape
    bm = _pick(m, (256, 128))
    return pl.pallas_call(
        _rmsnorm_body,
        out_shape=jax.ShapeDtypeStruct((m, d), F32),
        grid=(m // bm,),
        in_specs=[pl.BlockSpec((bm, d), lambda i: (i, 0)), pl.BlockSpec((1, d), lambda i: (0, 0))],
        out_specs=pl.BlockSpec((bm, d), lambda i: (i, 0)),
        compiler_params=_params(("arbitrary",), 6 * bm * d * 4),
        name="final_rmsnorm",
    )(x, g.reshape(1, d))


def _linear_body(*refs, n_pairs, swiglu, has_res, bm, n_x):
    a_refs = refs[:n_pairs]
    w_refs = refs[n_pairs:2 * n_pairs]
    pos = 2 * n_pairs
    if swiglu:
        w3_refs = refs[pos:pos + n_pairs]
        pos += n_pairs
    if has_res:
        res_ref, gate_ref = refs[pos], refs[pos + 1]
        pos += 2
    o_ref = refs[pos]

    def contract(ws):
        acc = None
        for a_ref, w_ref in zip(a_refs, ws):
            part = jnp.dot(a_ref[...], w_ref[...], preferred_element_type=F32)
            acc = part if acc is None else acc + part
        return acc

    acc = contract(w_refs)
    if swiglu:
        acc = _silu(acc) * contract(w3_refs)
    if has_res:
        is_c = _ctx_rows(pl.program_id(0), bm, n_x)
        gate = jnp.where(is_c, gate_ref[1:2, :], gate_ref[0:1, :])
        acc = res_ref[...] + gate * acc
    o_ref[...] = acc.astype(o_ref.dtype)


def linear(a_list, w_list, *, out_dtype, w3_list=None, res=None, mods=None, gate_blk=None, n_x=None,
           bm_opts=(1280, 1024, 640, 512, 256, 128), bn_opts=(512, 640, 256, 128), name="linear"):
    m = a_list[0].shape[0]
    n = w_list[0].shape[1]
    bm = _pick(m, bm_opts)
    bn = _pick(n, bn_opts)
    swiglu = w3_list is not None
    has_res = res is not None
    n_pairs = len(a_list)
    in_specs, args, vmem = [], [], 0
    for a in a_list:
        k = a.shape[1]
        in_specs.append(pl.BlockSpec((bm, k), lambda i, j: (i, 0)))
        args.append(a)
        vmem += 2 * bm * k * a.dtype.itemsize
    for ws in ([w_list, w3_list] if swiglu else [w_list]):
        for w in ws:
            k = w.shape[0]
            in_specs.append(pl.BlockSpec((k, bn), lambda i, j: (0, j)))
            args.append(w)
            vmem += 2 * k * bn * w.dtype.itemsize
    if has_res:
        in_specs.append(pl.BlockSpec((bm, bn), lambda i, j: (i, j)))
        args.append(res)
        gb = gate_blk * (n // bn)
        in_specs.append(pl.BlockSpec((8, bn), lambda i, j: (0, gb + j)))
        args.append(mods)
        vmem += 2 * bm * bn * 4
    vmem += 2 * bm * bn * jnp.dtype(out_dtype).itemsize + 3 * bm * bn * 4
    return pl.pallas_call(
        functools.partial(_linear_body, n_pairs=n_pairs, swiglu=swiglu, has_res=has_res, bm=bm,
                          n_x=m if n_x is None else n_x),
        out_shape=jax.ShapeDtypeStruct((m, n), out_dtype),
        grid=(m // bm, n // bn),
        in_specs=in_specs,
        out_specs=pl.BlockSpec((bm, bn), lambda i, j: (i, j)),
        compiler_params=_params(("arbitrary", "arbitrary"), vmem),
        name=name,
    )(*args)


def _qkv_body(cq_ref, ckv_ref, kr_ref, tq_ref, tk_ref, gq_ref, gkv_ref, wq_ref, wkv_ref,
              q_ref, k_ref, v_ref, *, heads):
    def rms(v, g):
        return (v * lax.rsqrt(jnp.mean(v * v, axis=-1, keepdims=True) + EPS) * g).astype(BF16)

    nq = rms(cq_ref[...], gq_ref[...])
    nkv = rms(ckv_ref[...], gkv_ref[...])
    tq = tq_ref[...]
    kp = kr_ref[...] * tk_ref[...]
    kr2 = (kp + pltpu.roll(kp, QK_ROPE, 1)).astype(BF16)
    for h in range(heads):
        lo = h * QK_HEAD
        t = jnp.dot(nq, wq_ref[:, lo:lo + QK_HEAD], preferred_element_type=F32)
        q_ref[:, lo:lo + QK_NOPE] = (t[:, :QK_NOPE] * Q_SCALE).astype(BF16)
        q_ref[:, lo + QK_NOPE:lo + QK_HEAD] = (t[:, QK_NOPE:] * tq).astype(BF16)
        t = jnp.dot(nkv, wkv_ref[:, lo:lo + QK_HEAD], preferred_element_type=F32)
        k_ref[:, lo:lo + QK_NOPE] = t[:, :QK_NOPE].astype(BF16)
        k_ref[:, lo + QK_NOPE:lo + QK_HEAD] = kr2
        v_ref[:, h * V_HEAD:(h + 1) * V_HEAD] = t[:, QK_NOPE:].astype(BF16)


def mla_qkv(u, off_cq, q_lora, off_ckv, kv_lora, off_kr, tq, tk, g_q, g_kv, wq, wkv, heads):
    m = u.shape[0]
    bm = _pick(m, (256, 128))
    rows = lambda i: (i, 0)
    fixed = lambda i: (0, 0)
    hw = heads * QK_HEAD
    vmem = 2 * bm * (q_lora + kv_lora + 3 * 128) * 4 + 2 * (q_lora + kv_lora) * hw * 2 \
        + 2 * bm * (2 * hw + heads * V_HEAD) * 2 + 4 * bm * hw * 4
    return pl.pallas_call(
        functools.partial(_qkv_body, heads=heads),
        out_shape=(jax.ShapeDtypeStruct((m, hw), BF16), jax.ShapeDtypeStruct((m, hw), BF16),
                   jax.ShapeDtypeStruct((m, heads * V_HEAD), BF16)),
        grid=(m // bm,),
        in_specs=[pl.BlockSpec((bm, q_lora), lambda i: (i, off_cq // q_lora)),
                  pl.BlockSpec((bm, kv_lora), lambda i: (i, off_ckv // kv_lora)),
                  pl.BlockSpec((bm, 128), lambda i: (i, off_kr // 128)),
                  pl.BlockSpec((bm, 128), rows), pl.BlockSpec((bm, 128), rows),
                  pl.BlockSpec((1, q_lora), fixed), pl.BlockSpec((1, kv_lora), fixed),
                  pl.BlockSpec((q_lora, hw), fixed), pl.BlockSpec((kv_lora, hw), fixed)],
        out_specs=(pl.BlockSpec((bm, hw), rows), pl.BlockSpec((bm, hw), rows),
                   pl.BlockSpec((bm, heads * V_HEAD), rows)),
        compiler_params=_params(("arbitrary",), vmem),
        name="mla_qkv",
    )(u, u, u, tq, tk, g_q.reshape(1, -1), g_kv.reshape(1, -1), wq, wkv)


def _flash_body(*refs, bk, n_blocks, aliased):
    q_ref, k_ref, vt_ref = refs[:3]
    o_ref = refs[4] if aliased else refs[3]
    s_refs, mb_refs, acc_ref, m_ref = refs[-6:-4], refs[-4:-2], refs[-2], refs[-1]
    q = q_ref[...]
    ones = jnp.ones((V_HEAD, bk), BF16)

    def scores(blk, buf):
        keys = k_ref[pl.ds(pl.multiple_of(blk * bk, bk), bk), :]
        s = lax.dot_general(keys, q, (((1,), (1,)), ((), ())), preferred_element_type=F32)
        s_refs[buf][...] = s
        mb_refs[buf][...] = jnp.max(s, axis=0, keepdims=True)

    def accumulate(blk, buf):
        m_old = m_ref[...]
        m_new = jnp.maximum(m_old, mb_refs[buf][...])
        p = jnp.exp2(s_refs[buf][...] - m_new).astype(BF16)
        vals = jnp.concatenate([vt_ref[blk], ones], axis=0)
        acc_ref[...] = jnp.exp2(m_old - m_new) * acc_ref[...] + jnp.dot(vals, p, preferred_element_type=F32)
        m_ref[...] = m_new

    m_ref[...] = jnp.full(m_ref.shape, -jnp.inf, F32)
    acc_ref[...] = jnp.zeros(acc_ref.shape, F32)
    scores(0, 0)
    n_pairs = (n_blocks - 1) // 2

    def pair(u, carry):
        accumulate(2 * u, 0)
        scores(2 * u + 1, 1)
        accumulate(2 * u + 1, 1)
        scores(2 * u + 2, 0)
        return carry

    lax.fori_loop(0, n_pairs, pair, 0)
    if (n_blocks - 1) % 2:
        accumulate(n_blocks - 2, 0)
        scores(n_blocks - 1, 1)
        accumulate(n_blocks - 1, 1)
    else:
        accumulate(n_blocks - 1, 0)
    acc = acc_ref[...]
    o_ref[...] = (acc[:V_HEAD, :] / acc[V_HEAD:, :]).T.astype(o_ref.dtype)


def attention(q, k, v, heads, *, q_rows, kv_row0, kv_rows, out=None):
    m = q.shape[0]
    aliased = out is not None
    q_row0 = kv_row0 if aliased else 0
    bq = _pick(q_rows, (2048, 512, 256, 128))
    bk = _pick(kv_rows, (1664, 1024, 512, 256, 128))
    n_blocks = kv_rows // bk
    assert q_row0 % bq == 0 and kv_row0 % kv_rows == 0
    qb0, kb0 = q_row0 // bq, kv_row0 // kv_rows
    vt = v[kv_row0:kv_row0 + kv_rows].reshape(n_blocks, bk, heads, V_HEAD).transpose(2, 0, 3, 1)
    once = pl.Buffered(1)
    in_specs = [pl.BlockSpec((bq, QK_HEAD), lambda h, i: (qb0 + i, h)),
                pl.BlockSpec((kv_rows, QK_HEAD), lambda h, i: (kb0, h), pipeline_mode=once),
                pl.BlockSpec((None, n_blocks, V_HEAD, bk), lambda h, i: (h, 0, 0, 0), pipeline_mode=once)]
    args = [q, k, vt]
    if aliased:
        in_specs.append(pl.BlockSpec(memory_space=pl.ANY))
        args.append(out)
    vmem = kv_rows * (QK_HEAD + V_HEAD) * 2 + 4 * bq * QK_HEAD * 2 + 2 * bq * bk * 4 + 2 * bq * min(bk, 1024) * 4
    return pl.pallas_call(
        functools.partial(_flash_body, bk=bk, n_blocks=n_blocks, aliased=aliased),
        out_shape=jax.ShapeDtypeStruct((m, heads * V_HEAD), BF16),
        grid=(heads, q_rows // bq),
        in_specs=in_specs,
        out_specs=pl.BlockSpec((bq, V_HEAD), lambda h, i: (qb0 + i, h)),
        scratch_shapes=[pltpu.VMEM((bk, bq), F32), pltpu.VMEM((bk, bq), F32),
                        pltpu.VMEM((1, bq), F32), pltpu.VMEM((1, bq), F32),
                        pltpu.VMEM((2 * V_HEAD, bq), F32), pltpu.VMEM((1, bq), F32)],
        input_output_aliases={3: 0} if aliased else {},
        compiler_params=_params(("arbitrary", "arbitrary"), vmem),
        name="attention_ctx" if aliased else "attention",
    )(*args)


def _seq_edges(tile, tiles_x, tiles_all):
    first = jnp.logical_or(tile == 0, tile == tiles_x)
    last = jnp.logical_or(tile == tiles_x - 1, tile == tiles_all - 1)
    return first, last


def _depthwise(ext_ref, w_ref, bm, taps, width, rc=64, cc=512):
    base = CONV_HALO - taps // 2
    rc = min(rc, bm)
    cc = min(cc, width)
    for r0 in range(0, bm, rc):
        for c0 in range(0, width, cc):
            acc = jnp.zeros((rc, cc), F32)
            for k in range(taps):
                acc = acc + ext_ref[r0 + base + k:r0 + base + k + rc, c0:c0 + cc] * w_ref[k:k + 1, c0:c0 + cc]
            yield r0, rc, c0, cc, acc


def _conformer_body(a_ref, b_ref, ap_ref, bp_ref, an_ref, bn_ref, w_ref, cb_ref, g_ref, be_ref, o_ref,
                    ext_ref, cv_ref, *, bm, taps, tiles_x, tiles_all):
    first, last = _seq_edges(pl.program_id(0), tiles_x, tiles_all)
    width = a_ref.shape[1]
    glu = lambda a, b: a * jax.nn.sigmoid(b)
    ext_ref[CONV_HALO:CONV_HALO + bm, :] = glu(a_ref[...], b_ref[...])
    ext_ref[0:CONV_HALO, :] = jnp.where(first, 0.0, glu(ap_ref[...], bp_ref[...]))
    ext_ref[CONV_HALO + bm:, :] = jnp.where(last, 0.0, glu(an_ref[...], bn_ref[...]))
    for r0, rc, c0, cc, acc in _depthwise(ext_ref, w_ref, bm, taps, width):
        cv_ref[r0:r0 + rc, c0:c0 + cc] = acc + cb_ref[:, c0:c0 + cc]
    y = cv_ref[...]
    mu = jnp.mean(y, axis=-1, keepdims=True)
    yc = y - mu
    yn = yc * lax.rsqrt(jnp.mean(yc * yc, axis=-1, keepdims=True) + EPS) * g_ref[...] + be_ref[...]
    o_ref[...] = _silu(yn).astype(o_ref.dtype)


def conformer_branch(u, ch, conv_w, conv_b, ln_g, ln_b, n_x):
    m = u.shape[0]
    taps = conv_w.shape[0]
    bm = _pick(math.gcd(n_x, m - n_x), (256, 128))
    hb = bm // CONV_HALO
    n_hb = m // CONV_HALO
    tiles_all = m // bm
    prev = lambda i: (jnp.maximum(i * hb - 1, 0), 0)
    prev_b = lambda i: (jnp.maximum(i * hb - 1, 0), 1)
    nxt = lambda i: (jnp.minimum((i + 1) * hb, n_hb - 1), 0)
    nxt_b = lambda i: (jnp.minimum((i + 1) * hb, n_hb - 1), 1)
    row = lambda i: (0, 0)
    vmem = 2 * 2 * (bm + 2 * CONV_HALO) * ch * 4 + 2 * bm * ch * 2 + (2 * bm + 2 * CONV_HALO) * ch * 4 \
        + 4 * bm * ch * 4
    return pl.pallas_call(
        functools.partial(_conformer_body, bm=bm, taps=taps, tiles_x=n_x // bm, tiles_all=tiles_all),
        out_shape=jax.ShapeDtypeStruct((m, ch), BF16),
        grid=(tiles_all,),
        in_specs=[pl.BlockSpec((bm, ch), lambda i: (i, 0)), pl.BlockSpec((bm, ch), lambda i: (i, 1)),
                  pl.BlockSpec((CONV_HALO, ch), prev), pl.BlockSpec((CONV_HALO, ch), prev_b),
                  pl.BlockSpec((CONV_HALO, ch), nxt), pl.BlockSpec((CONV_HALO, ch), nxt_b),
                  pl.BlockSpec((taps, ch), row), pl.BlockSpec((1, ch), row),
                  pl.BlockSpec((1, ch), row), pl.BlockSpec((1, ch), row)],
        out_specs=pl.BlockSpec((bm, ch), lambda i: (i, 0)),
        scratch_shapes=[pltpu.VMEM((bm + 2 * CONV_HALO, ch), F32), pltpu.VMEM((bm, ch), F32)],
        compiler_params=_params(("arbitrary",), vmem),
        name="conformer_branch",
    )(u, u, u, u, u, u, conv_w, conv_b.reshape(1, ch), ln_g.reshape(1, ch), ln_b.reshape(1, ch))


def _ssd_conv_body(x_ref, xp_ref, xn_ref, w_ref, cb_ref, o_ref, ext_ref, *, bm, taps, tiles_x, tiles_all):
    first, last = _seq_edges(pl.program_id(1), tiles_x, tiles_all)
    width = x_ref.shape[1]
    ext_ref[CONV_HALO:CONV_HALO + bm, :] = x_ref[...]
    ext_ref[0:CONV_HALO, :] = jnp.where(first, 0.0, xp_ref[...])
    ext_ref[CONV_HALO + bm:, :] = jnp.where(last, 0.0, xn_ref[...])
    for r0, rc, c0, cc, acc in _depthwise(ext_ref, w_ref, bm, taps, width):
        o_ref[r0:r0 + rc, c0:c0 + cc] = _silu(acc + cb_ref[:, c0:c0 + cc]).astype(o_ref.dtype)


def ssd_conv(xbc, conv_w, conv_b, n_x):
    m, ch = xbc.shape
    taps = conv_w.shape[0]
    bm = _pick(math.gcd(n_x, m - n_x), (256, 128))
    bc = _pick(ch, (1024, 512, 256, 128))
    hb = bm // CONV_HALO
    n_hb = m // CONV_HALO
    tiles_all = m // bm
    vmem = 2 * (2 * bm + 2 * CONV_HALO) * bc * 4 + (bm + 2 * CONV_HALO) * bc * 4 + 2 * bm * bc * 4
    return pl.pallas_call(
        functools.partial(_ssd_conv_body, bm=bm, taps=taps, tiles_x=n_x // bm, tiles_all=tiles_all),
        out_shape=jax.ShapeDtypeStruct((m, ch), BF16),
        grid=(ch // bc, tiles_all),
        in_specs=[pl.BlockSpec((bm, bc), lambda c, i: (i, c)),
                  pl.BlockSpec((CONV_HALO, bc), lambda c, i: (jnp.maximum(i * hb - 1, 0), c)),
                  pl.BlockSpec((CONV_HALO, bc), lambda c, i: (jnp.minimum((i + 1) * hb, n_hb - 1), c)),
                  pl.BlockSpec((taps, bc), lambda c, i: (0, c)),
                  pl.BlockSpec((1, bc), lambda c, i: (0, c))],
        out_specs=pl.BlockSpec((bm, bc), lambda c, i: (i, c)),
        scratch_shapes=[pltpu.VMEM((bm + 2 * CONV_HALO, bc), F32)],
        compiler_params=_params(("arbitrary", "arbitrary"), vmem),
        name="ssd_conv",
    )(xbc, xbc, xbc, conv_w, conv_b.reshape(1, ch))


def _split3(v):
    hi = v.astype(BF16)
    r1 = v - hi.astype(F32)
    mid = r1.astype(BF16)
    lo = (r1 - mid.astype(F32)).astype(BF16)
    return hi, mid, lo


def _dt_body(raw_ref, bias_ref, alog_ref, dt_ref, ac_ref, *, n_heads):
    v = raw_ref[...] + bias_ref[...]
    dt = jnp.maximum(v, 0.0) + jnp.log1p(jnp.exp(-jnp.abs(v)))
    dt_ref[...] = dt
    dta = dt * (-jnp.exp(alog_ref[...]))
    ii = lax.broadcasted_iota(jnp.int32, (CHUNK, CHUNK), 0)
    jj = lax.broadcasted_iota(jnp.int32, (CHUNK, CHUNK), 1)
    lower = jnp.where(ii >= jj, 1.0, 0.0).astype(BF16)
    upper = jnp.where(ii <= jj, 1.0, 0.0).astype(BF16)
    pre = jnp.zeros(dta.shape, F32)
    suf = jnp.zeros(dta.shape, F32)
    for part in _split3(dta)[::-1]:
        pre = pre + jnp.dot(lower, part, preferred_element_type=F32)
        suf = suf + jnp.dot(upper, part, preferred_element_type=F32)
    col = lax.broadcasted_iota(jnp.int32, dta.shape, 1)
    ac_ref[...] = jnp.where(col < n_heads, pre, suf)


def ssd_dt(raw, dt_bias, a_log):
    m, w = raw.shape
    blk = pl.BlockSpec((CHUNK, w), lambda i: (i, 0))
    vec = pl.BlockSpec((1, w), lambda i: (0, 0))
    return pl.pallas_call(
        functools.partial(_dt_body, n_heads=w // 2),
        out_shape=(jax.ShapeDtypeStruct((m, w), F32), jax.ShapeDtypeStruct((m, w), F32)),
        grid=(m // CHUNK,),
        in_specs=[blk, vec, vec],
        out_specs=(blk, blk),
        compiler_params=_params(("arbitrary",), 16 * CHUNK * w * 4),
        name="ssd_dt",
    )(raw, dt_bias.reshape(1, w), a_log.reshape(1, w))


def _expand_cols(v_rows, sel):
    out = None
    for part in _split3(v_rows)[::-1]:
        r = lax.dot_general(part, sel, (((0,), (0,)), ((), ())), preferred_element_type=F32)
        out = r if out is None else out + r
    return out


def _head_selector(hg, width_per_head):
    head = lax.broadcasted_iota(jnp.int32, (hg, hg * width_per_head), 0)
    col = lax.broadcasted_iota(jnp.int32, (hg, hg * width_per_head), 1)
    lo = head * width_per_head
    return jnp.where(col >= lo, jnp.where(col < lo + width_per_head, 1.0, 0.0), 0.0).astype(BF16)


def _ssd_body(x_ref, b_ref, c_ref, dt_ref, ac_ref, y_ref, st_ref, *, hg, p):
    d = pl.program_id(0)

    @pl.when(pl.program_id(2) == 0)
    def _():
        st_ref[...] = jnp.zeros(st_ref.shape, F32)

    b_t = b_ref[...].astype(F32).T.astype(BF16)
    cmat = c_ref[...].astype(BF16)
    cb = jnp.dot(cmat, b_t, preferred_element_type=F32)
    ii = lax.broadcasted_iota(jnp.int32, (CHUNK, CHUNK), 0)
    jj = lax.broadcasted_iota(jnp.int32, (CHUNK, CHUNK), 1)
    mask = jnp.where(d == 0, ii - jj, jj - ii) >= 0
    low_half = jj < p
    ac_rows = ac_ref[...]
    ac_wide = _expand_cols(ac_rows, _head_selector(hg, CHUNK))
    dt_x = _expand_cols(dt_ref[...], _head_selector(hg, p))
    ac_x = jnp.concatenate(
        [jnp.where(low_half, ac_wide[:, (2 * v) * CHUNK:(2 * v + 1) * CHUNK],
                   ac_wide[:, (2 * v + 1) * CHUNK:(2 * v + 2) * CHUNK]) for v in range(hg // 2)], axis=1)
    last = jnp.where(d == 0, ac_x[CHUNK - 1:CHUNK, :], ac_x[0:1, :])
    x = x_ref[...].astype(F32)
    st = st_ref[...]
    y_state = jnp.dot(cmat, st.astype(BF16), preferred_element_type=F32) * jnp.exp(ac_x)
    dtx = (dt_x * x).astype(BF16)
    zero = jnp.zeros((CHUNK, CHUNK), BF16)
    for v in range(hg // 2):
        pair = slice(v * CHUNK, (v + 1) * CHUNK)
        mixes = []
        for h in (2 * v, 2 * v + 1):
            seg = ac_wide[:, h * CHUNK:(h + 1) * CHUNK] - ac_rows[h:h + 1, :]
            mixes.append((cb * jnp.exp(jnp.where(mask, seg, -jnp.inf))).astype(BF16))
        dtx_pair = dtx[:, pair]
        rhs = jnp.concatenate([jnp.where(low_half, dtx_pair, zero), jnp.where(low_half, zero, dtx_pair)], axis=0)
        y_in = jnp.dot(jnp.concatenate(mixes, axis=1), rhs, preferred_element_type=F32)
        y_ref[:, pair] = (y_in + y_state[:, pair]).astype(y_ref.dtype)
    wx = (jnp.exp(last - ac_x) * dt_x * x).astype(BF16)
    st_ref[...] = st * jnp.exp(last) + jnp.dot(b_t, wx, preferred_element_type=F32)


def ssd_scan(xbc_c, dt_rows, ac_rows, d_inner, n_x):
    m = xbc_c.shape[0]
    groups = SSM_GROUPS
    gw = d_inner // groups
    hg = gw // SSM_HEADDIM
    assert SSM_HEADDIM * 2 == CHUNK and hg % 2 == 0
    n_xc = n_x // CHUNK
    n_all = m // CHUNK
    n_cc = n_all - n_xc
    b_blk0 = d_inner // D_STATE
    c_blk0 = b_blk0 + groups

    def chunk(d, t):
        fwd = jnp.where(t < n_cc, n_xc + t, t - n_cc)
        return jnp.where(d == 0, fwd, n_all - 1 - t)

    rows = pl.BlockSpec((None, None, None, hg, CHUNK), lambda d, g, t: (d, g, chunk(d, t), 0, 0))
    vmem = 4 * CHUNK * gw * 4 + 4 * CHUNK * D_STATE * 4 + D_STATE * gw * 4 + 12 * CHUNK * gw * 4 \
        + 2 * CHUNK * hg * CHUNK * 4
    return pl.pallas_call(
        functools.partial(_ssd_body, hg=hg, p=SSM_HEADDIM),
        out_shape=jax.ShapeDtypeStruct((2, m, d_inner), BF16),
        grid=(2, groups, n_all),
        in_specs=[pl.BlockSpec((CHUNK, gw), lambda d, g, t: (chunk(d, t), g)),
                  pl.BlockSpec((CHUNK, D_STATE), lambda d, g, t: (chunk(d, t), b_blk0 + g)),
                  pl.BlockSpec((CHUNK, D_STATE), lambda d, g, t: (chunk(d, t), c_blk0 + g)),
                  rows, rows],
        out_specs=pl.BlockSpec((None, CHUNK, gw), lambda d, g, t: (d, chunk(d, t), g)),
        scratch_shapes=[pltpu.VMEM((D_STATE, gw), F32)],
        compiler_params=_params(("arbitrary", "arbitrary", "arbitrary"), vmem),
        name="ssd_scan",
    )(xbc_c, xbc_c, xbc_c, dt_rows, ac_rows)


def _gate_norm_body(yf_ref, yb_ref, xs_ref, z_ref, dsk_ref, g_ref, o_ref):
    y = yf_ref[...].astype(F32) + yb_ref[...].astype(F32) + dsk_ref[...] * xs_ref[...].astype(F32)
    y = y * _silu(z_ref[...].astype(F32))
    o_ref[...] = (y * lax.rsqrt(jnp.mean(y * y, axis=-1, keepdims=True) + EPS) * g_ref[...]).astype(o_ref.dtype)


def ssd_gate_norm(y2, xbc_c, z, d_skip_cols, norm_g, n_x):
    di = z.shape[1]
    bm = _pick(n_x, (128,))
    blk = pl.BlockSpec((bm, di), lambda i: (i, 0))
    vec = pl.BlockSpec((1, di), lambda i: (0, 0))
    return pl.pallas_call(
        _gate_norm_body,
        out_shape=jax.ShapeDtypeStruct((n_x, di), BF16),
        grid=(n_x // bm,),
        in_specs=[pl.BlockSpec((None, bm, di), lambda i: (0, i, 0)),
                  pl.BlockSpec((None, bm, di), lambda i: (1, i, 0)), blk, blk, vec, vec],
        out_specs=blk,
        compiler_params=_params(("arbitrary",), 2 * bm * di * 18 + 4 * bm * di * 4),
        name="ssd_gate_norm",
    )(y2, y2, xbc_c, z, d_skip_cols.reshape(1, di), norm_g.reshape(1, di))


def _rope_perm():
    d = jnp.arange(QK_ROPE)
    first = (d % (QK_ROPE // 2)) < QK_ROPE // 4
    return jnp.where(first, d + QK_ROPE // 4, d - QK_ROPE // 4), jnp.where(first, -1.0, 1.0).astype(F32)


def _rope_tables(n_x, n_ctx):
    rows = n_x // GRID_W
    row = jnp.repeat(jnp.arange(rows, dtype=F32), GRID_W)
    col = jnp.tile(jnp.arange(GRID_W, dtype=F32), rows)
    quarter = QK_ROPE // 4
    freqs = 1.0 / (ROPE_THETA ** (jnp.arange(quarter, dtype=F32) / quarter))
    ar = row[:, None] * freqs
    ac = col[:, None] * freqs
    ang = jnp.concatenate([ar, ar, ac, ac], axis=-1)
    tab = jnp.concatenate([jnp.cos(ang), jnp.sin(ang)], axis=-1)
    ctx_tab = jnp.concatenate([jnp.ones((n_ctx, QK_ROPE), F32), jnp.zeros((n_ctx, QK_ROPE), F32)], axis=-1)
    return jnp.concatenate([tab, ctx_tab], axis=0)


def _ffn(xa, mods, norm_g, w1, w3, w2, n_x):
    h = norm_mod(xa, norm_g, mods, 3, 4, n_x)
    g = linear([h], [w1.astype(BF16)], w3_list=[w3.astype(BF16)], out_dtype=BF16, bn_opts=(256, 128),
               name="ffn_up")
    return linear([g], [w2.astype(BF16)], out_dtype=F32, res=xa, mods=mods, gate_blk=5, n_x=n_x,
                  bm_opts=(640, 512, 256, 128), bn_opts=(256, 128), name="ffn_down")


def _attn_conv_layer(xa, mods, n_x, norm1_g, w_in, g_q, w_uq, g_kv, w_ukv, conv_w, conv_b, ln_g, ln_b, w_o):
    m = xa.shape[0]
    q_lora, kv_lora, ch = g_q.shape[0], g_kv.shape[0], conv_b.shape[0]
    heads = w_uq.shape[1] // (QK_NOPE + QK_ROPE)
    perm, sign = _rope_perm()
    c0 = q_lora + kv_lora + QK_ROPE
    w_kr = w_in[:, q_lora + kv_lora:c0]
    w_in_r = jnp.concatenate([w_in[:, c0:], w_in[:, :q_lora + kv_lora], w_kr, w_kr[:, perm] * sign],
                             axis=1).astype(BF16)
    off_cq, off_ckv, off_kr = 2 * ch, 2 * ch + q_lora, 2 * ch + q_lora + kv_lora
    wq3 = w_uq.reshape(q_lora, heads, QK_NOPE + QK_ROPE)
    wq_rope = wq3[:, :, QK_NOPE:]
    wq = jnp.concatenate([wq3, wq_rope[:, :, perm] * sign], axis=-1).reshape(q_lora, heads * QK_HEAD).astype(BF16)
    wkv = w_ukv.astype(BF16)
    tk = _rope_tables(n_x, m - n_x)
    tq = tk * Q_SCALE

    h = norm_mod(xa, norm1_g, mods, 0, 1, n_x)
    u = linear([h], [w_in_r], out_dtype=F32, bn_opts=(640, 512, 256, 128), name="attn_in_proj")
    q, k, v = mla_qkv(u, off_cq, q_lora, off_ckv, kv_lora, off_kr, tq, tk, g_q, g_kv, wq, wkv, heads)
    cv = conformer_branch(u, ch, conv_w, conv_b, ln_g, ln_b, n_x)
    o = attention(q, k, v, heads, q_rows=n_x, kv_row0=0, kv_rows=m)
    o = attention(q, k, v, heads, q_rows=m - n_x, kv_row0=n_x, kv_rows=m - n_x, out=o)
    hv = heads * V_HEAD
    return linear([o, cv], [w_o[:hv].astype(BF16), w_o[hv:].astype(BF16)], out_dtype=F32, res=xa, mods=mods,
                  gate_blk=2, n_x=n_x, name="attn_out_proj")


def _ssd_layer(xa, mods, n_x, norm1_g, w_in, conv_w, conv_b, dt_bias, a_log, d_skip, norm_g, w_out):
    m = xa.shape[0]
    d_inner = norm_g.shape[0]
    n_heads = d_skip.shape[0]
    groups = SSM_GROUPS
    hg = n_heads // groups
    xbc_w = conv_b.shape[0]
    h = norm_mod(xa, norm1_g, mods, 0, 1, n_x)
    z = linear([h], [w_in[:, :d_inner].astype(BF16)], out_dtype=BF16, name="ssd_in_z")
    xbc = linear([h], [w_in[:, d_inner:d_inner + xbc_w].astype(BF16)], out_dtype=F32, name="ssd_in_xbc")
    dt_raw = linear([h], [w_in[:, d_inner + xbc_w:].astype(BF16)], out_dtype=F32, name="ssd_in_dt")
    xbc_c = ssd_conv(xbc, conv_w, conv_b, n_x)
    dt, ac = ssd_dt(dt_raw, dt_bias.reshape(-1), a_log.reshape(-1))
    to_rows = lambda t: t.reshape(m // CHUNK, CHUNK, 2, groups, hg).transpose(2, 3, 0, 4, 1)
    y2 = ssd_scan(xbc_c, to_rows(dt), to_rows(ac), d_inner, n_x)
    yn = ssd_gate_norm(y2, xbc_c, z, jnp.repeat(d_skip, SSM_HEADDIM), norm_g, n_x)
    return linear([yn], [w_out.astype(BF16)], out_dtype=F32, res=xa, mods=mods, gate_blk=2,
                  bm_opts=(1024, 512, 256, 128), bn_opts=(256, 128), name="ssd_out_proj")


def kernel(x, c, ctx, c_ctx, ada_w, ada_b, norm1_g, norm2_g, a_w_in, a_g_q, a_w_uq, a_g_kv, a_w_ukv, a_conv_w,
           a_conv_b, a_ln_g, a_ln_b, a_w_o, m_w_in, m_conv_w, m_conv_b, m_dt_bias, m_a_log, m_d, m_norm_g,
           m_w_out, ffn_w1, ffn_w3, ffn_w2, final_g):
    assert x.shape[0] == 1 and ada_w.shape[0] == 2
    n_x, d = x.shape[1], x.shape[2]
    xa = jnp.concatenate([x[0], ctx[0]], axis=0)
    cond8 = jnp.zeros((8, d), F32).at[0].set(c[0]).at[1].set(c_ctx)
    mods = adaln(cond8, ada_w, ada_b)

    xa = _attn_conv_layer(xa, mods[0], n_x, norm1_g[0], a_w_in[0], a_g_q[0], a_w_uq[0], a_g_kv[0], a_w_ukv[0],
                          a_conv_w[0], a_conv_b[0], a_ln_g[0], a_ln_b[0], a_w_o[0])
    xa = _ffn(xa, mods[0], norm2_g[0], ffn_w1[0], ffn_w3[0], ffn_w2[0], n_x)
    xl = _ssd_layer(xa, mods[1], n_x, norm1_g[1], m_w_in[0], m_conv_w[0], m_conv_b[0], m_dt_bias[0], m_a_log[0],
                    m_d[0], m_norm_g[0], m_w_out[0])
    xl = _ffn(xl, mods[1], norm2_g[1], ffn_w1[1], ffn_w3[1], ffn_w2[1], xl.shape[0])
    return final_rmsnorm(xl, final_g)[None]
```

```python
import functools
import math

import jax
import jax.numpy as jnp
from jax import lax
from jax.experimental import pallas as pl
from jax.experimental.pallas import tpu as pltpu

F32 = jnp.float32
BF16 = jnp.bfloat16
EPS = 1e-6

GRID_W = 64
ROPE_THETA = 10000.0
QK_NOPE = 128
QK_ROPE = 64
V_HEAD = 128
QK_HEAD = QK_NOPE + 2 * QK_ROPE
ATTN_SCALE = 1.0 / math.sqrt(QK_NOPE + QK_ROPE)
Q_SCALE = ATTN_SCALE * math.log2(math.e)
SSM_GROUPS = 8
SSM_HEADDIM = 64
D_STATE = 128
CHUNK = 128
CONV_HALO = 16

V7X_VMEM_BYTES = 64 * 1024 * 1024
VMEM_CAP = V7X_VMEM_BYTES - 6 * 1024 * 1024


def _pick(n, candidates):
    for c in candidates:
        if c <= n and n % c == 0:
            return c
    return n


def _params(semantics, vmem_bytes, flags=None):
    limit = int(min(max(vmem_bytes * 1.25 + (4 << 20), 24 << 20), VMEM_CAP))
    return pltpu.CompilerParams(dimension_semantics=semantics, vmem_limit_bytes=limit, flags=flags)


def _silu(v):
    return v * jax.nn.sigmoid(v)


def _ctx_rows(tile, bm, n_x):
    rows = tile * bm + lax.broadcasted_iota(jnp.int32, (bm, 1), 0)
    return rows >= n_x


def _adaln_body(c_ref, w_ref, b_ref, o_ref):
    a = _silu(c_ref[...]).astype(BF16)
    o_ref[...] = jnp.dot(a, w_ref[...].astype(BF16), preferred_element_type=F32) + b_ref[...]


def adaln(cond8, ada_w, ada_b):
    depth, d, n = ada_w.shape
    tn = _pick(n, (512, 256, 128))
    return pl.pallas_call(
        _adaln_body,
        out_shape=jax.ShapeDtypeStruct((depth, 8, n), F32),
        grid=(depth, n // tn),
        in_specs=[pl.BlockSpec((8, d), lambda l, j: (0, 0)),
                  pl.BlockSpec((None, d, tn), lambda l, j: (l, 0, j)),
                  pl.BlockSpec((None, 1, tn), lambda l, j: (l, 0, j))],
        out_specs=pl.BlockSpec((None, 8, tn), lambda l, j: (l, 0, j)),
        compiler_params=_params(("arbitrary", "arbitrary"), 3 * d * tn * 4),
        name="adaln",
    )(cond8, ada_w, ada_b.reshape(depth, 1, n))


def _norm_mod_body(x_ref, g_ref, sh_ref, sc_ref, o_ref, *, bm, n_x):
    x = x_ref[...]
    y = x * lax.rsqrt(jnp.mean(x * x, axis=-1, keepdims=True) + EPS) * g_ref[...]
    is_c = _ctx_rows(pl.program_id(0), bm, n_x)
    sc = jnp.where(is_c, sc_ref[1:2, :], sc_ref[0:1, :])
    sh = jnp.where(is_c, sh_ref[1:2, :], sh_ref[0:1, :])
    o_ref[...] = (y * (1.0 + sc) + sh).astype(o_ref.dtype)


def norm_mod(x, g, mods, shift_blk, scale_blk, n_x):
    m, d = x.shape
    bm = _pick(m, (320, 256, 128))
    return pl.pallas_call(
        functools.partial(_norm_mod_body, bm=bm, n_x=n_x),
        out_shape=jax.ShapeDtypeStruct((m, d), BF16),
        grid=(m // bm,),
        in_specs=[pl.BlockSpec((bm, d), lambda i: (i, 0)),
                  pl.BlockSpec((1, d), lambda i: (0, 0)),
                  pl.BlockSpec((8, d), lambda i: (0, shift_blk)),
                  pl.BlockSpec((8, d), lambda i: (0, scale_blk))],
        out_specs=pl.BlockSpec((bm, d), lambda i: (i, 0)),
        compiler_params=_params(("arbitrary",), 2 * bm * d * 6 + 6 * bm * d * 4),
        name="norm_mod",
    )(x, g.reshape(1, d), mods, mods)


def _rmsnorm_body(x_ref, g_ref, o_ref):
    x = x_ref[...]
    o_ref[...] = x * lax.rsqrt(jnp.mean(x * x, axis=-1, keepdims=True) + EPS) * g_ref[...]


def final_rmsnorm(x, g):
    m, d = x.shape
    bm = _pick(m, (256, 128))
    return pl.pallas_call(
        _rmsnorm_body,
        out_shape=jax.ShapeDtypeStruct((m, d), F32),
        grid=(m // bm,),
        in_specs=[pl.BlockSpec((bm, d), lambda i: (i, 0)), pl.BlockSpec((1, d), lambda i: (0, 0))],
        out_specs=pl.BlockSpec((bm, d), lambda i: (i, 0)),
        compiler_params=_params(("arbitrary",), 6 * bm * d * 4),
        name="final_rmsnorm",
    )(x, g.reshape(1, d))


def _linear_body(*refs, n_pairs, swiglu, has_res, bm, n_x):
    a_refs = refs[:n_pairs]
    w_refs = refs[n_pairs:2 * n_pairs]
    pos = 2 * n_pairs
    if swiglu:
        w3_refs = refs[pos:pos + n_pairs]
        pos += n_pairs
    if has_res:
        res_ref, gate_ref = refs[pos], refs[pos + 1]
        pos += 2
    o_ref = refs[pos]

    def contract(ws):
        acc = None
        for a_ref, w_ref in zip(a_refs, ws):
            part = jnp.dot(a_ref[...], w_ref[...].astype(BF16), preferred_element_type=F32)
            acc = part if acc is None else acc + part
        return acc

    acc = contract(w_refs)
    if swiglu:
        acc = _silu(acc) * contract(w3_refs)
    if has_res:
        is_c = _ctx_rows(pl.program_id(0), bm, n_x)
        gate = jnp.where(is_c, gate_ref[1:2, :], gate_ref[0:1, :])
        acc = res_ref[...] + gate * acc
    o_ref[...] = acc.astype(o_ref.dtype)


def linear(a_list, w_list, *, out_dtype, w3_list=None, res=None, mods=None, gate_blk=None, n_x=None, n=None, col0=0,
           bm_opts=(1280, 1024, 640, 512, 256, 128), bn_opts=(512, 640, 256, 128), name="linear"):
    m = a_list[0].shape[0]
    n = w_list[0].shape[1] if n is None else n
    bm = _pick(m, bm_opts)
    bn = _pick(math.gcd(n, col0) if col0 else n, bn_opts)
    cb0 = col0 // bn
    swiglu = w3_list is not None
    has_res = res is not None
    n_pairs = len(a_list)
    in_specs, args, vmem = [], [], 0
    for a in a_list:
        k = a.shape[1]
        in_specs.append(pl.BlockSpec((bm, k), lambda i, j: (i, 0)))
        args.append(a)
        vmem += 2 * bm * k * a.dtype.itemsize
    for ws in ([w_list, w3_list] if swiglu else [w_list]):
        for w in ws:
            k = w.shape[0]
            in_specs.append(pl.BlockSpec((k, bn), lambda i, j: (0, cb0 + j)))
            args.append(w)
            vmem += 2 * k * bn * w.dtype.itemsize + (k * bn * 2 if w.dtype == F32 else 0)
    if has_res:
        in_specs.append(pl.BlockSpec((bm, bn), lambda i, j: (i, j)))
        args.append(res)
        gb = gate_blk * (n // bn)
        in_specs.append(pl.BlockSpec((8, bn), lambda i, j: (0, gb + j)))
        args.append(mods)
        vmem += 2 * bm * bn * 4
    vmem += 2 * bm * bn * jnp.dtype(out_dtype).itemsize + 3 * bm * bn * 4
    return pl.pallas_call(
        functools.partial(_linear_body, n_pairs=n_pairs, swiglu=swiglu, has_res=has_res, bm=bm,
                          n_x=m if n_x is None else n_x),
        out_shape=jax.ShapeDtypeStruct((m, n), out_dtype),
        grid=(m // bm, n // bn),
        in_specs=in_specs,
        out_specs=pl.BlockSpec((bm, bn), lambda i, j: (i, j)),
        compiler_params=_params(("arbitrary", "arbitrary"), vmem),
        name=name,
    )(*args)


def _qkv_body(cq_ref, ckv_ref, kr_ref, tq_ref, tk_ref, gq_ref, gkv_ref, wq_ref, wkv_ref,
              q_ref, k_ref, v_ref, *, heads):
    def rms(v, g):
        return (v * lax.rsqrt(jnp.mean(v * v, axis=-1, keepdims=True) + EPS) * g).astype(BF16)

    nq = rms(cq_ref[...], gq_ref[...])
    nkv = rms(ckv_ref[...], gkv_ref[...])
    tq = tq_ref[...]
    kp = kr_ref[...] * tk_ref[...]
    kr2 = (kp + pltpu.roll(kp, QK_ROPE, 1)).astype(BF16)
    for h in range(heads):
        lo = h * QK_HEAD
        t = jnp.dot(nq, wq_ref[:, lo:lo + QK_HEAD], preferred_element_type=F32)
        q_ref[:, lo:lo + QK_NOPE] = (t[:, :QK_NOPE] * Q_SCALE).astype(BF16)
        q_ref[:, lo + QK_NOPE:lo + QK_HEAD] = (t[:, QK_NOPE:] * tq).astype(BF16)
        t = jnp.dot(nkv, wkv_ref[:, lo:lo + QK_HEAD], preferred_element_type=F32)
        k_ref[:, lo:lo + QK_NOPE] = t[:, :QK_NOPE].astype(BF16)
        k_ref[:, lo + QK_NOPE:lo + QK_HEAD] = kr2
        v_ref[:, h * V_HEAD:(h + 1) * V_HEAD] = t[:, QK_NOPE:].astype(BF16)


def mla_qkv(u, off_cq, q_lora, off_ckv, kv_lora, off_kr, tq, tk, g_q, g_kv, wq, wkv, heads):
    m = u.shape[0]
    bm = _pick(m, (256, 128))
    rows = lambda i: (i, 0)
    fixed = lambda i: (0, 0)
    hw = heads * QK_HEAD
    vmem = 2 * bm * (q_lora + kv_lora + 3 * 128) * 4 + 2 * (q_lora + kv_lora) * hw * 2 \
        + 2 * bm * (2 * hw + heads * V_HEAD) * 2 + 4 * bm * hw * 4
    return pl.pallas_call(
        functools.partial(_qkv_body, heads=heads),
        out_shape=(jax.ShapeDtypeStruct((m, hw), BF16), jax.ShapeDtypeStruct((m, hw), BF16),
                   jax.ShapeDtypeStruct((m, heads * V_HEAD), BF16)),
        grid=(m // bm,),
        in_specs=[pl.BlockSpec((bm, q_lora), lambda i: (i, off_cq // q_lora)),
                  pl.BlockSpec((bm, kv_lora), lambda i: (i, off_ckv // kv_lora)),
                  pl.BlockSpec((bm, 128), lambda i: (i, off_kr // 128)),
                  pl.BlockSpec((bm, 128), rows), pl.BlockSpec((bm, 128), rows),
                  pl.BlockSpec((1, q_lora), fixed), pl.BlockSpec((1, kv_lora), fixed),
                  pl.BlockSpec((q_lora, hw), fixed), pl.BlockSpec((kv_lora, hw), fixed)],
        out_specs=(pl.BlockSpec((bm, hw), rows), pl.BlockSpec((bm, hw), rows),
                   pl.BlockSpec((bm, heads * V_HEAD), rows)),
        compiler_params=_params(("arbitrary",), vmem),
        name="mla_qkv",
    )(u, u, u, tq, tk, g_q.reshape(1, -1), g_kv.reshape(1, -1), wq, wkv)


def _flash_body(*refs, bk, n_blocks, aliased):
    q_ref, k_ref, vt_ref = refs[:3]
    o_ref = refs[4] if aliased else refs[3]
    s_refs, mb_refs, acc_ref, m_ref, l_ref = refs[-7:-5], refs[-5:-3], refs[-3], refs[-2], refs[-1]
    q = q_ref[...]

    def scores(blk, buf):
        keys = k_ref[pl.ds(pl.multiple_of(blk * bk, bk), bk), :]
        s = lax.dot_general(keys, q, (((1,), (1,)), ((), ())), preferred_element_type=F32)
        s_refs[buf][...] = s
        mb_refs[buf][...] = jnp.max(s, axis=0, keepdims=True)

    def accumulate(blk, buf):
        m_old = m_ref[...]
        m_new = jnp.maximum(m_old, mb_refs[buf][...])
        alpha = jnp.exp2(m_old - m_new)
        p = jnp.exp2(s_refs[buf][...] - m_new)
        l_ref[...] = alpha * l_ref[...] + jnp.sum(p, axis=0, keepdims=True)
        acc_ref[...] = alpha * acc_ref[...] + jnp.dot(vt_ref[blk], p.astype(BF16), preferred_element_type=F32)
        m_ref[...] = m_new

    m_ref[...] = jnp.full(m_ref.shape, -jnp.inf, F32)
    l_ref[...] = jnp.zeros(l_ref.shape, F32)
    acc_ref[...] = jnp.zeros(acc_ref.shape, F32)
    scores(0, 0)
    n_pairs = (n_blocks - 1) // 2

    def pair(u, carry):
        accumulate(2 * u, 0)
        scores(2 * u + 1, 1)
        accumulate(2 * u + 1, 1)
        scores(2 * u + 2, 0)
        return carry

    lax.fori_loop(0, n_pairs, pair, 0)
    if (n_blocks - 1) % 2:
        accumulate(n_blocks - 2, 0)
        scores(n_blocks - 1, 1)
        accumulate(n_blocks - 1, 1)
    else:
        accumulate(n_blocks - 1, 0)
    o_ref[...] = (acc_ref[...] / l_ref[...]).T.astype(o_ref.dtype)


def attention(q, k, v, heads, *, q_rows, kv_row0, kv_rows, out=None):
    m = q.shape[0]
    aliased = out is not None
    q_row0 = kv_row0 if aliased else 0
    bq = _pick(q_rows, (2048, 512, 256, 128))
    bk = _pick(kv_rows, (1664, 1024, 512, 256, 128))
    n_blocks = kv_rows // bk
    assert q_row0 % bq == 0 and kv_row0 % kv_rows == 0
    qb0, kb0 = q_row0 // bq, kv_row0 // kv_rows
    vt = v[kv_row0:kv_row0 + kv_rows].reshape(n_blocks, bk, heads, V_HEAD).transpose(2, 0, 3, 1)
    once = pl.Buffered(1)
    in_specs = [pl.BlockSpec((bq, QK_HEAD), lambda h, i: (qb0 + i, h)),
                pl.BlockSpec((kv_rows, QK_HEAD), lambda h, i: (kb0, h), pipeline_mode=once),
                pl.BlockSpec((None, n_blocks, V_HEAD, bk), lambda h, i: (h, 0, 0, 0), pipeline_mode=once)]
    args = [q, k, vt]
    if aliased:
        in_specs.append(pl.BlockSpec(memory_space=pl.ANY))
        args.append(out)
    vmem = kv_rows * (QK_HEAD + V_HEAD) * 2 + 4 * bq * QK_HEAD * 2 + 2 * bq * bk * 4 + 2 * bq * min(bk, 1024) * 4
    return pl.pallas_call(
        functools.partial(_flash_body, bk=bk, n_blocks=n_blocks, aliased=aliased),
        out_shape=jax.ShapeDtypeStruct((m, heads * V_HEAD), BF16),
        grid=(heads, q_rows // bq),
        in_specs=in_specs,
        out_specs=pl.BlockSpec((bq, V_HEAD), lambda h, i: (qb0 + i, h)),
        scratch_shapes=[pltpu.VMEM((bk, bq), F32), pltpu.VMEM((bk, bq), F32),
                        pltpu.VMEM((1, bq), F32), pltpu.VMEM((1, bq), F32),
                        pltpu.VMEM((V_HEAD, bq), F32), pltpu.VMEM((1, bq), F32), pltpu.VMEM((1, bq), F32)],
        input_output_aliases={3: 0} if aliased else {},
        compiler_params=_params(("arbitrary", "arbitrary"), vmem),
        name="attention_ctx" if aliased else "attention",
    )(*args)


def _seq_edges(tile, tiles_x, tiles_all):
    first = jnp.logical_or(tile == 0, tile == tiles_x)
    last = jnp.logical_or(tile == tiles_x - 1, tile == tiles_all - 1)
    return first, last


def _depthwise(ext_ref, w_ref, bm, taps, width, rc=128, cc=128):
    base = CONV_HALO - taps // 2
    rc = min(rc, bm)
    cc = min(cc, width)
    span = rc + 2 * CONV_HALO
    by_shift = {}
    for k in range(taps):
        by_shift.setdefault((base + k) % 8, []).append(k)
    for r0 in range(0, bm, rc):
        for c0 in range(0, width, cc):
            window = ext_ref[r0:r0 + span, c0:c0 + cc]
            acc = jnp.zeros((rc, cc), F32)
            for shift, ks in sorted(by_shift.items()):
                rolled = window if shift == 0 else pltpu.roll(window, span - shift, 0)
                for k in ks:
                    a0 = base + k - shift
                    acc = acc + rolled[a0:a0 + rc, :] * w_ref[k:k + 1, c0:c0 + cc]
            yield r0, rc, c0, cc, acc


def _conformer_body(a_ref, b_ref, ap_ref, bp_ref, an_ref, bn_ref, w_ref, cb_ref, g_ref, be_ref, o_ref,
                    ext_ref, cv_ref, *, bm, taps, tiles_x, tiles_all):
    first, last = _seq_edges(pl.program_id(0), tiles_x, tiles_all)
    width = a_ref.shape[1]
    glu = lambda a, b: a * jax.nn.sigmoid(b)
    ext_ref[CONV_HALO:CONV_HALO + bm, :] = glu(a_ref[...], b_ref[...])
    ext_ref[0:CONV_HALO, :] = jnp.where(first, 0.0, glu(ap_ref[...], bp_ref[...]))
    ext_ref[CONV_HALO + bm:, :] = jnp.where(last, 0.0, glu(an_ref[...], bn_ref[...]))
    for r0, rc, c0, cc, acc in _depthwise(ext_ref, w_ref, bm, taps, width):
        cv_ref[r0:r0 + rc, c0:c0 + cc] = acc + cb_ref[:, c0:c0 + cc]
    y = cv_ref[...]
    mu = jnp.mean(y, axis=-1, keepdims=True)
    yc = y - mu
    yn = yc * lax.rsqrt(jnp.mean(yc * yc, axis=-1, keepdims=True) + EPS) * g_ref[...] + be_ref[...]
    o_ref[...] = _silu(yn).astype(o_ref.dtype)


def conformer_branch(u, ch, conv_w, conv_b, ln_g, ln_b, n_x):
    m = u.shape[0]
    taps = conv_w.shape[0]
    bm = _pick(math.gcd(n_x, m - n_x), (256, 128))
    hb = bm // CONV_HALO
    n_hb = m // CONV_HALO
    tiles_all = m // bm
    prev = lambda i: (jnp.maximum(i * hb - 1, 0), 0)
    prev_b = lambda i: (jnp.maximum(i * hb - 1, 0), 1)
    nxt = lambda i: (jnp.minimum((i + 1) * hb, n_hb - 1), 0)
    nxt_b = lambda i: (jnp.minimum((i + 1) * hb, n_hb - 1), 1)
    row = lambda i: (0, 0)
    vmem = 2 * 2 * (bm + 2 * CONV_HALO) * ch * 4 + 2 * bm * ch * 2 + (2 * bm + 2 * CONV_HALO) * ch * 4 \
        + 4 * bm * ch * 4
    return pl.pallas_call(
        functools.partial(_conformer_body, bm=bm, taps=taps, tiles_x=n_x // bm, tiles_all=tiles_all),
        out_shape=jax.ShapeDtypeStruct((m, ch), BF16),
        grid=(tiles_all,),
        in_specs=[pl.BlockSpec((bm, ch), lambda i: (i, 0)), pl.BlockSpec((bm, ch), lambda i: (i, 1)),
                  pl.BlockSpec((CONV_HALO, ch), prev), pl.BlockSpec((CONV_HALO, ch), prev_b),
                  pl.BlockSpec((CONV_HALO, ch), nxt), pl.BlockSpec((CONV_HALO, ch), nxt_b),
                  pl.BlockSpec((taps, ch), row), pl.BlockSpec((1, ch), row),
                  pl.BlockSpec((1, ch), row), pl.BlockSpec((1, ch), row)],
        out_specs=pl.BlockSpec((bm, ch), lambda i: (i, 0)),
        scratch_shapes=[pltpu.VMEM((bm + 2 * CONV_HALO, ch), F32), pltpu.VMEM((bm, ch), F32)],
        compiler_params=_params(("arbitrary",), vmem),
        name="conformer_branch",
    )(u, u, u, u, u, u, conv_w, conv_b.reshape(1, ch), ln_g.reshape(1, ch), ln_b.reshape(1, ch))


def _ssd_conv_body(x_ref, xp_ref, xn_ref, w_ref, cb_ref, o_ref, ext_ref, *, bm, taps, tiles_x, tiles_all):
    first, last = _seq_edges(pl.program_id(1), tiles_x, tiles_all)
    width = x_ref.shape[1]
    ext_ref[CONV_HALO:CONV_HALO + bm, :] = x_ref[...]
    ext_ref[0:CONV_HALO, :] = jnp.where(first, 0.0, xp_ref[...])
    ext_ref[CONV_HALO + bm:, :] = jnp.where(last, 0.0, xn_ref[...])
    for r0, rc, c0, cc, acc in _depthwise(ext_ref, w_ref, bm, taps, width):
        o_ref[r0:r0 + rc, c0:c0 + cc] = _silu(acc + cb_ref[:, c0:c0 + cc]).astype(o_ref.dtype)


def ssd_conv(xbc, conv_w, conv_b, n_x):
    m, ch = xbc.shape
    taps = conv_w.shape[0]
    bm = _pick(math.gcd(n_x, m - n_x), (256, 128))
    bc = _pick(ch, (1024, 512, 256, 128))
    hb = bm // CONV_HALO
    n_hb = m // CONV_HALO
    tiles_all = m // bm
    vmem = 2 * (2 * bm + 2 * CONV_HALO) * bc * 4 + (bm + 2 * CONV_HALO) * bc * 4 + 2 * bm * bc * 4
    return pl.pallas_call(
        functools.partial(_ssd_conv_body, bm=bm, taps=taps, tiles_x=n_x // bm, tiles_all=tiles_all),
        out_shape=jax.ShapeDtypeStruct((m, ch), BF16),
        grid=(ch // bc, tiles_all),
        in_specs=[pl.BlockSpec((bm, bc), lambda c, i: (i, c)),
                  pl.BlockSpec((CONV_HALO, bc), lambda c, i: (jnp.maximum(i * hb - 1, 0), c)),
                  pl.BlockSpec((CONV_HALO, bc), lambda c, i: (jnp.minimum((i + 1) * hb, n_hb - 1), c)),
                  pl.BlockSpec((taps, bc), lambda c, i: (0, c)),
                  pl.BlockSpec((1, bc), lambda c, i: (0, c))],
        out_specs=pl.BlockSpec((bm, bc), lambda c, i: (i, c)),
        scratch_shapes=[pltpu.VMEM((bm + 2 * CONV_HALO, bc), F32)],
        compiler_params=_params(("arbitrary", "arbitrary"), vmem),
        name="ssd_conv",
    )(xbc, xbc, xbc, conv_w, conv_b.reshape(1, ch))


def _split3(v):
    hi = v.astype(BF16)
    r1 = v - hi.astype(F32)
    mid = r1.astype(BF16)
    lo = (r1 - mid.astype(F32)).astype(BF16)
    return hi, mid, lo


def _dt_body(raw_ref, bias_ref, alog_ref, dt_ref, ac_ref, *, n_heads):
    v = raw_ref[...] + bias_ref[...]
    dt = jnp.maximum(v, 0.0) + jnp.log1p(jnp.exp(-jnp.abs(v)))
    dt_ref[...] = dt
    dta = dt * (-jnp.exp(alog_ref[...]))
    ii = lax.broadcasted_iota(jnp.int32, (CHUNK, CHUNK), 0)
    jj = lax.broadcasted_iota(jnp.int32, (CHUNK, CHUNK), 1)
    lower = jnp.where(ii >= jj, 1.0, 0.0).astype(BF16)
    upper = jnp.where(ii <= jj, 1.0, 0.0).astype(BF16)
    pre = jnp.zeros(dta.shape, F32)
    suf = jnp.zeros(dta.shape, F32)
    for part in _split3(dta)[::-1]:
        pre = pre + jnp.dot(lower, part, preferred_element_type=F32)
        suf = suf + jnp.dot(upper, part, preferred_element_type=F32)
    col = lax.broadcasted_iota(jnp.int32, dta.shape, 1)
    ac_ref[...] = jnp.where(col < n_heads, pre, suf)


def ssd_dt(raw, dt_bias, a_log):
    m, w = raw.shape
    blk = pl.BlockSpec((CHUNK, w), lambda i: (i, 0))
    vec = pl.BlockSpec((1, w), lambda i: (0, 0))
    return pl.pallas_call(
        functools.partial(_dt_body, n_heads=w // 2),
        out_shape=(jax.ShapeDtypeStruct((m, w), F32), jax.ShapeDtypeStruct((m, w), F32)),
        grid=(m // CHUNK,),
        in_specs=[blk, vec, vec],
        out_specs=(blk, blk),
        compiler_params=_params(("arbitrary",), 16 * CHUNK * w * 4),
        name="ssd_dt",
    )(raw, dt_bias.reshape(1, w), a_log.reshape(1, w))


def _expand_cols(v_rows, sel, pieces=3):
    out = None
    for part in _split3(v_rows)[:pieces][::-1]:
        r = lax.dot_general(part, sel, (((0,), (0,)), ((), ())), preferred_element_type=F32)
        out = r if out is None else out + r
    return out


def _head_selector(hg, width_per_head):
    head = lax.broadcasted_iota(jnp.int32, (hg, hg * width_per_head), 0)
    col = lax.broadcasted_iota(jnp.int32, (hg, hg * width_per_head), 1)
    lo = head * width_per_head
    return jnp.where(col >= lo, jnp.where(col < lo + width_per_head, 1.0, 0.0), 0.0).astype(BF16)


def _ssd_body(x_ref, b_ref, c_ref, dt_ref, ac_ref, accol_ref, y_ref, st_ref, wx_ref, dec_ref, *, hg, p):
    d = pl.program_id(0)

    @pl.when(pl.program_id(2) == 0)
    def _():
        st_ref[...] = jnp.zeros(st_ref.shape, F32)

    b_t = b_ref[...].astype(F32).T.astype(BF16)
    cmat = c_ref[...].astype(BF16)
    cb = jnp.dot(cmat, b_t, preferred_element_type=F32)
    ii = lax.broadcasted_iota(jnp.int32, (CHUNK, CHUNK), 0)
    jj = lax.broadcasted_iota(jnp.int32, (CHUNK, CHUNK), 1)
    mask = jnp.where(d == 0, ii - jj, jj - ii) >= 0
    low_half = jj < p
    ac_rows = ac_ref[...]
    ac_cols = accol_ref[...]
    dt_x = _expand_cols(dt_ref[...], _head_selector(hg, p), pieces=2)
    dx = dt_x * x_ref[...].astype(F32)
    st = st_ref[...]
    y_state = jnp.dot(cmat, st.astype(BF16), preferred_element_type=F32)
    zero = jnp.zeros((CHUNK, CHUNK), BF16)
    for v in range(hg // 2):
        pair = slice(v * CHUNK, (v + 1) * CHUNK)
        heads = (2 * v, 2 * v + 1)
        wide = [jnp.broadcast_to(ac_cols[:, h:h + 1], (CHUNK, CHUNK)) for h in heads]
        ac_pair = jnp.where(low_half, wide[0], wide[1])
        last = jnp.where(d == 0, ac_pair[CHUNK - 1:CHUNK, :], ac_pair[0:1, :])
        mixes = [(cb * jnp.exp(jnp.where(mask, w - ac_rows[h:h + 1, :], -jnp.inf))).astype(BF16)
                 for w, h in zip(wide, heads)]
        dx_pair = dx[:, pair]
        dtx_pair = dx_pair.astype(BF16)
        rhs = jnp.concatenate([jnp.where(low_half, dtx_pair, zero), jnp.where(low_half, zero, dtx_pair)], axis=0)
        y_in = jnp.dot(jnp.concatenate(mixes, axis=1), rhs, preferred_element_type=F32)
        y_ref[:, pair] = (y_in + y_state[:, pair] * jnp.exp(ac_pair)).astype(y_ref.dtype)
        wx_ref[:, pair] = (jnp.exp(last - ac_pair) * dx_pair).astype(BF16)
        dec_ref[:, pair] = jnp.exp(last)
    st_ref[...] = st * dec_ref[...] + jnp.dot(b_t, wx_ref[...], preferred_element_type=F32)


def ssd_scan(xbc_c, dt_rows, ac_rows, ac_cols, d_inner, n_x):
    m = xbc_c.shape[0]
    groups = SSM_GROUPS
    gw = d_inner // groups
    hg = gw // SSM_HEADDIM
    assert SSM_HEADDIM * 2 == CHUNK and hg % 2 == 0
    n_xc = n_x // CHUNK
    n_all = m // CHUNK
    n_cc = n_all - n_xc
    b_blk0 = d_inner // D_STATE
    c_blk0 = b_blk0 + groups

    def chunk(d, t):
        fwd = jnp.where(t < n_cc, n_xc + t, t - n_cc)
        return jnp.where(d == 0, fwd, n_all - 1 - t)

    rows = pl.BlockSpec((None, None, None, hg, CHUNK), lambda d, g, t: (d, g, chunk(d, t), 0, 0))
    vmem = 4 * CHUNK * gw * 4 + 4 * CHUNK * D_STATE * 4 + D_STATE * gw * 4 + 12 * CHUNK * gw * 4 \
        + 2 * CHUNK * hg * CHUNK * 4
    return pl.pallas_call(
        functools.partial(_ssd_body, hg=hg, p=SSM_HEADDIM),
        out_shape=jax.ShapeDtypeStruct((2, m, d_inner), BF16),
        grid=(2, groups, n_all),
        in_specs=[pl.BlockSpec((CHUNK, gw), lambda d, g, t: (chunk(d, t), g)),
                  pl.BlockSpec((CHUNK, D_STATE), lambda d, g, t: (chunk(d, t), b_blk0 + g)),
                  pl.BlockSpec((CHUNK, D_STATE), lambda d, g, t: (chunk(d, t), c_blk0 + g)),
                  rows, rows,
                  pl.BlockSpec((None, None, CHUNK, hg), lambda d, g, t: (d, g, chunk(d, t), 0))],
        out_specs=pl.BlockSpec((None, CHUNK, gw), lambda d, g, t: (d, chunk(d, t), g)),
        scratch_shapes=[pltpu.VMEM((D_STATE, gw), F32), pltpu.VMEM((CHUNK, gw), BF16), pltpu.VMEM((1, gw), F32)],
        compiler_params=_params(("arbitrary", "arbitrary", "arbitrary"), vmem),
        name="ssd_scan",
    )(xbc_c, xbc_c, xbc_c, dt_rows, ac_rows, ac_cols)


def _gate_norm_body(yf_ref, yb_ref, xs_ref, z_ref, dsk_ref, g_ref, o_ref):
    y = yf_ref[...].astype(F32) + yb_ref[...].astype(F32) + dsk_ref[...] * xs_ref[...].astype(F32)
    y = y * _silu(z_ref[...].astype(F32))
    o_ref[...] = (y * lax.rsqrt(jnp.mean(y * y, axis=-1, keepdims=True) + EPS) * g_ref[...]).astype(o_ref.dtype)


def ssd_gate_norm(y2, xbc_c, z, d_skip_cols, norm_g, n_x):
    di = z.shape[1]
    bm = _pick(n_x, (128,))
    blk = pl.BlockSpec((bm, di), lambda i: (i, 0))
    vec = pl.BlockSpec((1, di), lambda i: (0, 0))
    return pl.pallas_call(
        _gate_norm_body,
        out_shape=jax.ShapeDtypeStruct((n_x, di), BF16),
        grid=(n_x // bm,),
        in_specs=[pl.BlockSpec((None, bm, di), lambda i: (0, i, 0)),
                  pl.BlockSpec((None, bm, di), lambda i: (1, i, 0)), blk, blk, vec, vec],
        out_specs=blk,
        compiler_params=_params(("arbitrary",), 2 * bm * di * 18 + 4 * bm * di * 4),
        name="ssd_gate_norm",
    )(y2, y2, xbc_c, z, d_skip_cols.reshape(1, di), norm_g.reshape(1, di))


def _rope_perm():
    d = jnp.arange(QK_ROPE)
    first = (d % (QK_ROPE // 2)) < QK_ROPE // 4
    return jnp.where(first, d + QK_ROPE // 4, d - QK_ROPE // 4), jnp.where(first, -1.0, 1.0).astype(F32)


def _rope_tables(n_x, n_ctx):
    rows = n_x // GRID_W
    row = jnp.repeat(jnp.arange(rows, dtype=F32), GRID_W)
    col = jnp.tile(jnp.arange(GRID_W, dtype=F32), rows)
    quarter = QK_ROPE // 4
    freqs = 1.0 / (ROPE_THETA ** (jnp.arange(quarter, dtype=F32) / quarter))
    ar = row[:, None] * freqs
    ac = col[:, None] * freqs
    ang = jnp.concatenate([ar, ar, ac, ac], axis=-1)
    tab = jnp.concatenate([jnp.cos(ang), jnp.sin(ang)], axis=-1)
    ctx_tab = jnp.concatenate([jnp.ones((n_ctx, QK_ROPE), F32), jnp.zeros((n_ctx, QK_ROPE), F32)], axis=-1)
    return jnp.concatenate([tab, ctx_tab], axis=0)


def _ffn(xa, mods, norm_g, w1, w3, w2, n_x):
    h = norm_mod(xa, norm_g, mods, 3, 4, n_x)
    g = linear([h], [w1], w3_list=[w3], out_dtype=BF16, bn_opts=(256, 128), name="ffn_up")
    return linear([g], [w2.astype(BF16)], out_dtype=F32, res=xa, mods=mods, gate_blk=5, n_x=n_x,
                  bm_opts=(640, 512, 256, 128), bn_opts=(256, 128), name="ffn_down")


def _attn_conv_layer(xa, mods, n_x, norm1_g, w_in, g_q, w_uq, g_kv, w_ukv, conv_w, conv_b, ln_g, ln_b, w_o):
    m = xa.shape[0]
    q_lora, kv_lora, ch = g_q.shape[0], g_kv.shape[0], conv_b.shape[0]
    heads = w_uq.shape[1] // (QK_NOPE + QK_ROPE)
    perm, sign = _rope_perm()
    c0 = q_lora + kv_lora + QK_ROPE
    w_kr = w_in[:, q_lora + kv_lora:c0]
    w_in_r = jnp.concatenate([w_in[:, c0:], w_in[:, :q_lora + kv_lora], w_kr, w_kr[:, perm] * sign],
                             axis=1).astype(BF16)
    off_cq, off_ckv, off_kr = 2 * ch, 2 * ch + q_lora, 2 * ch + q_lora + kv_lora
    wq3 = w_uq.reshape(q_lora, heads, QK_NOPE + QK_ROPE)
    wq_rope = wq3[:, :, QK_NOPE:]
    wq = jnp.concatenate([wq3, wq_rope[:, :, perm] * sign], axis=-1).reshape(q_lora, heads * QK_HEAD).astype(BF16)
    wkv = w_ukv.astype(BF16)
    tk = _rope_tables(n_x, m - n_x)
    tq = tk * Q_SCALE

    h = norm_mod(xa, norm1_g, mods, 0, 1, n_x)
    u = linear([h], [w_in_r], out_dtype=F32, bn_opts=(640, 512, 256, 128), name="attn_in_proj")
    q, k, v = mla_qkv(u, off_cq, q_lora, off_ckv, kv_lora, off_kr, tq, tk, g_q, g_kv, wq, wkv, heads)
    cv = conformer_branch(u, ch, conv_w, conv_b, ln_g, ln_b, n_x)
    o = attention(q, k, v, heads, q_rows=n_x, kv_row0=0, kv_rows=m)
    o = attention(q, k, v, heads, q_rows=m - n_x, kv_row0=n_x, kv_rows=m - n_x, out=o)
    hv = heads * V_HEAD
    return linear([o, cv], [w_o[:hv].astype(BF16), w_o[hv:].astype(BF16)], out_dtype=F32, res=xa, mods=mods,
                  gate_blk=2, n_x=n_x, name="attn_out_proj")


def _ssd_layer(xa, mods, n_x, norm1_g, w_in, conv_w, conv_b, dt_bias, a_log, d_skip, norm_g, w_out):
    m = xa.shape[0]
    d_inner = norm_g.shape[0]
    n_heads = d_skip.shape[0]
    groups = SSM_GROUPS
    hg = n_heads // groups
    xbc_w = conv_b.shape[0]
    h = norm_mod(xa, norm1_g, mods, 0, 1, n_x)
    z = linear([h], [w_in], n=d_inner, out_dtype=BF16, name="ssd_in_z")
    xbc = linear([h], [w_in], n=xbc_w, col0=d_inner, out_dtype=F32, name="ssd_in_xbc")
    dt_raw = linear([h], [w_in], n=2 * n_heads, col0=d_inner + xbc_w, out_dtype=F32, name="ssd_in_dt")
    xbc_c = ssd_conv(xbc, conv_w, conv_b, n_x)
    dt, ac = ssd_dt(dt_raw, dt_bias.reshape(-1), a_log.reshape(-1))
    to_rows = lambda t: t.reshape(m // CHUNK, CHUNK, 2, groups, hg).transpose(2, 3, 0, 4, 1)
    ac_cols = ac.reshape(m, 2, groups, hg).transpose(1, 2, 0, 3)
    y2 = ssd_scan(xbc_c, to_rows(dt), to_rows(ac), ac_cols, d_inner, n_x)
    yn = ssd_gate_norm(y2, xbc_c, z, jnp.repeat(d_skip, SSM_HEADDIM), norm_g, n_x)
    return linear([yn], [w_out.astype(BF16)], out_dtype=F32, res=xa, mods=mods, gate_blk=2,
                  bm_opts=(1024, 512, 256, 128), bn_opts=(256, 128), name="ssd_out_proj")


def kernel(x, c, ctx, c_ctx, ada_w, ada_b, norm1_g, norm2_g, a_w_in, a_g_q, a_w_uq, a_g_kv, a_w_ukv, a_conv_w,
           a_conv_b, a_ln_g, a_ln_b, a_w_o, m_w_in, m_conv_w, m_conv_b, m_dt_bias, m_a_log, m_d, m_norm_g,
           m_w_out, ffn_w1, ffn_w3, ffn_w2, final_g):
    assert x.shape[0] == 1 and ada_w.shape[0] == 2
    n_x, d = x.shape[1], x.shape[2]
    xa = jnp.concatenate([x[0], ctx[0]], axis=0)
    cond8 = jnp.zeros((8, d), F32).at[0].set(c[0]).at[1].set(c_ctx)
    mods = adaln(cond8, ada_w, ada_b)

    xa = _attn_conv_layer(xa, mods[0], n_x, norm1_g[0], a_w_in[0], a_g_q[0], a_w_uq[0], a_g_kv[0], a_w_ukv[0],
                          a_conv_w[0], a_conv_b[0], a_ln_g[0], a_ln_b[0], a_w_o[0])
    xa = _ffn(xa, mods[0], norm2_g[0], ffn_w1[0], ffn_w3[0], ffn_w2[0], n_x)
    xl = _ssd_layer(xa, mods[1], n_x, norm1_g[1], m_w_in[0], m_conv_w[0], m_conv_b[0], m_dt_bias[0], m_a_log[0],
                    m_d[0], m_norm_g[0], m_w_out[0])
    xl = _ffn(xl, mods[1], norm2_g[1], ffn_w1[1], ffn_w3[1], ffn_w2[1], xl.shape[0])
    return final_rmsnorm(xl, final_g)[None]
```

```python
import functools
import math

import jax
import jax.numpy as jnp
from jax import lax
from jax.experimental import pallas as pl
from jax.experimental.pallas import tpu as pltpu

F32 = jnp.float32
BF16 = jnp.bfloat16
EPS = 1e-6

GRID_W = 64
ROPE_THETA = 10000.0
QK_NOPE = 128
QK_ROPE = 64
V_HEAD = 128
QK_HEAD = QK_NOPE + 2 * QK_ROPE
ATTN_SCALE = 1.0 / math.sqrt(QK_NOPE + QK_ROPE)
Q_SCALE = ATTN_SCALE * math.log2(math.e)
SSM_GROUPS = 8
SSM_HEADDIM = 64
D_STATE = 128
CHUNK = 128
CONV_HALO = 16

V7X_VMEM_BYTES = 64 * 1024 * 1024
VMEM_CAP = V7X_VMEM_BYTES - 6 * 1024 * 1024


def _pick(n, candidates):
    for c in candidates:
        if c <= n and n % c == 0:
            return c
    return n


def _params(semantics, vmem_bytes, flags=None):
    limit = int(min(max(vmem_bytes * 1.25 + (4 << 20), 24 << 20), VMEM_CAP))
    return pltpu.CompilerParams(dimension_semantics=semantics, vmem_limit_bytes=limit, flags=flags)


def _silu(v):
    return v * jax.nn.sigmoid(v)


def _ctx_rows(tile, bm, n_x):
    rows = tile * bm + lax.broadcasted_iota(jnp.int32, (bm, 1), 0)
    return rows >= n_x


def _adaln_body(c_ref, w_ref, b_ref, o_ref):
    a = _silu(c_ref[...]).astype(BF16)
    o_ref[...] = jnp.dot(a, w_ref[...].astype(BF16), preferred_element_type=F32) + b_ref[...]


def adaln(cond8, ada_w, ada_b):
    depth, d, n = ada_w.shape
    tn = _pick(n, (512, 256, 128))
    return pl.pallas_call(
        _adaln_body,
        out_shape=jax.ShapeDtypeStruct((depth, 8, n), F32),
        grid=(depth, n // tn),
        in_specs=[pl.BlockSpec((8, d), lambda l, j: (0, 0)),
                  pl.BlockSpec((None, d, tn), lambda l, j: (l, 0, j)),
                  pl.BlockSpec((None, 1, tn), lambda l, j: (l, 0, j))],
        out_specs=pl.BlockSpec((None, 8, tn), lambda l, j: (l, 0, j)),
        compiler_params=_params(("arbitrary", "arbitrary"), 3 * d * tn * 4),
        name="adaln",
    )(cond8, ada_w, ada_b.reshape(depth, 1, n))


def _norm_mod_body(x_ref, g_ref, sh_ref, sc_ref, o_ref, *, bm, n_x):
    x = x_ref[...]
    y = x * lax.rsqrt(jnp.mean(x * x, axis=-1, keepdims=True) + EPS) * g_ref[...]
    is_c = _ctx_rows(pl.program_id(0), bm, n_x)
    sc = jnp.where(is_c, sc_ref[1:2, :], sc_ref[0:1, :])
    sh = jnp.where(is_c, sh_ref[1:2, :], sh_ref[0:1, :])
    o_ref[...] = (y * (1.0 + sc) + sh).astype(o_ref.dtype)


def norm_mod(x, g, mods, shift_blk, scale_blk, n_x):
    m, d = x.shape
    bm = _pick(m, (320, 256, 128))
    return pl.pallas_call(
        functools.partial(_norm_mod_body, bm=bm, n_x=n_x),
        out_shape=jax.ShapeDtypeStruct((m, d), BF16),
        grid=(m // bm,),
        in_specs=[pl.BlockSpec((bm, d), lambda i: (i, 0)),
                  pl.BlockSpec((1, d), lambda i: (0, 0)),
                  pl.BlockSpec((8, d), lambda i: (0, shift_blk)),
                  pl.BlockSpec((8, d), lambda i: (0, scale_blk))],
        out_specs=pl.BlockSpec((bm, d), lambda i: (i, 0)),
        compiler_params=_params(("arbitrary",), 2 * bm * d * 6 + 6 * bm * d * 4),
        name="norm_mod",
    )(x, g.reshape(1, d), mods, mods)


def _rmsnorm_body(x_ref, g_ref, o_ref):
    x = x_ref[...]
    o_ref[...] = x * lax.rsqrt(jnp.mean(x * x, axis=-1, keepdims=True) + EPS) * g_ref[...]


def final_rmsnorm(x, g):
    m, d = x.shape
    bm = _pick(m, (256, 128))
    return pl.pallas_call(
        _rmsnorm_body,
        out_shape=jax.ShapeDtypeStruct((m, d), F32),
        grid=(m // bm,),
        in_specs=[pl.BlockSpec((bm, d), lambda i: (i, 0)), pl.BlockSpec((1, d), lambda i: (0, 0))],
        out_specs=pl.BlockSpec((bm, d), lambda i: (i, 0)),
        compiler_params=_params(("arbitrary",), 6 * bm * d * 4),
        name="final_rmsnorm",
    )(x, g.reshape(1, d))


def _linear_body(*refs, n_pairs, swiglu, has_res, bm, n_x):
    a_refs = refs[:n_pairs]
    w_refs = refs[n_pairs:2 * n_pairs]
    pos = 2 * n_pairs
    if swiglu:
        w3_refs = refs[pos:pos + n_pairs]
        pos += n_pairs
    if has_res:
        res_ref, gate_ref = refs[pos], refs[pos + 1]
        pos += 2
    o_ref = refs[pos]

    def contract(ws):
        acc = None
        for a_ref, w_ref in zip(a_refs, ws):
            part = jnp.dot(a_ref[...], w_ref[...].astype(BF16), preferred_element_type=F32)
            acc = part if acc is None else acc + part
        return acc

    acc = contract(w_refs)
    if swiglu:
        acc = _silu(acc) * contract(w3_refs)
    if has_res:
        is_c = _ctx_rows(pl.program_id(0), bm, n_x)
        gate = jnp.where(is_c, gate_ref[1:2, :], gate_ref[0:1, :])
        acc = res_ref[...] + gate * acc
    o_ref[...] = acc.astype(o_ref.dtype)


def linear(a_list, w_list, *, out_dtype, w3_list=None, res=None, mods=None, gate_blk=None, n_x=None, n=None, col0=0,
           layer=0, bm_opts=(1280, 1024, 640, 512, 256, 128), bn_opts=(512, 640, 256, 128), name="linear"):
    m = a_list[0].shape[0]
    n = w_list[0].shape[-1] if n is None else n
    bm = _pick(m, bm_opts)
    bn = _pick(math.gcd(n, col0) if col0 else n, bn_opts)
    cb0 = col0 // bn
    swiglu = w3_list is not None
    has_res = res is not None
    n_pairs = len(a_list)
    in_specs, args, vmem = [], [], 0
    for a in a_list:
        k = a.shape[1]
        in_specs.append(pl.BlockSpec((bm, k), lambda i, j: (i, 0)))
        args.append(a)
        vmem += 2 * bm * k * a.dtype.itemsize
    for ws in ([w_list, w3_list] if swiglu else [w_list]):
        for w in ws:
            k = w.shape[-2]
            if w.ndim == 3:
                in_specs.append(pl.BlockSpec((None, k, bn), lambda i, j: (layer, 0, cb0 + j)))
            else:
                in_specs.append(pl.BlockSpec((k, bn), lambda i, j: (0, cb0 + j)))
            args.append(w)
            vmem += 2 * k * bn * w.dtype.itemsize + (k * bn * 2 if w.dtype == F32 else 0)
    if has_res:
        in_specs.append(pl.BlockSpec((bm, bn), lambda i, j: (i, j)))
        args.append(res)
        gb = gate_blk * (n // bn)
        in_specs.append(pl.BlockSpec((8, bn), lambda i, j: (0, gb + j)))
        args.append(mods)
        vmem += 2 * bm * bn * 4
    vmem += 2 * bm * bn * jnp.dtype(out_dtype).itemsize + 3 * bm * bn * 4
    return pl.pallas_call(
        functools.partial(_linear_body, n_pairs=n_pairs, swiglu=swiglu, has_res=has_res, bm=bm,
                          n_x=m if n_x is None else n_x),
        out_shape=jax.ShapeDtypeStruct((m, n), out_dtype),
        grid=(m // bm, n // bn),
        in_specs=in_specs,
        out_specs=pl.BlockSpec((bm, bn), lambda i, j: (i, j)),
        compiler_params=_params(("arbitrary", "arbitrary"), vmem),
        name=name,
    )(*args)


def _qkv_body(cq_ref, ckv_ref, kr_ref, tq_ref, tk_ref, gq_ref, gkv_ref, wq_ref, wkv_ref,
              q_ref, k_ref, v_ref, *, heads):
    def rms(v, g):
        return (v * lax.rsqrt(jnp.mean(v * v, axis=-1, keepdims=True) + EPS) * g).astype(BF16)

    nq = rms(cq_ref[...], gq_ref[...])
    nkv = rms(ckv_ref[...], gkv_ref[...])
    tq = tq_ref[...]
    kp = kr_ref[...] * tk_ref[...]
    kr2 = (kp + pltpu.roll(kp, QK_ROPE, 1)).astype(BF16)
    for h in range(heads):
        lo = h * QK_HEAD
        t = jnp.dot(nq, wq_ref[:, lo:lo + QK_HEAD], preferred_element_type=F32)
        q_ref[:, lo:lo + QK_NOPE] = (t[:, :QK_NOPE] * Q_SCALE).astype(BF16)
        q_ref[:, lo + QK_NOPE:lo + QK_HEAD] = (t[:, QK_NOPE:] * tq).astype(BF16)
        t = jnp.dot(nkv, wkv_ref[:, lo:lo + QK_HEAD], preferred_element_type=F32)
        k_ref[:, lo:lo + QK_NOPE] = t[:, :QK_NOPE].astype(BF16)
        k_ref[:, lo + QK_NOPE:lo + QK_HEAD] = kr2
        v_ref[:, h * V_HEAD:(h + 1) * V_HEAD] = t[:, QK_NOPE:].astype(BF16)


def mla_qkv(u, off_cq, q_lora, off_ckv, kv_lora, off_kr, tq, tk, g_q, g_kv, wq, wkv, heads):
    m = u.shape[0]
    bm = _pick(m, (256, 128))
    rows = lambda i: (i, 0)
    fixed = lambda i: (0, 0)
    hw = heads * QK_HEAD
    vmem = 2 * bm * (q_lora + kv_lora + 3 * 128) * 4 + 2 * (q_lora + kv_lora) * hw * 2 \
        + 2 * bm * (2 * hw + heads * V_HEAD) * 2 + 4 * bm * hw * 4
    return pl.pallas_call(
        functools.partial(_qkv_body, heads=heads),
        out_shape=(jax.ShapeDtypeStruct((m, hw), BF16), jax.ShapeDtypeStruct((m, hw), BF16),
                   jax.ShapeDtypeStruct((m, heads * V_HEAD), BF16)),
        grid=(m // bm,),
        in_specs=[pl.BlockSpec((bm, q_lora), lambda i: (i, off_cq // q_lora)),
                  pl.BlockSpec((bm, kv_lora), lambda i: (i, off_ckv // kv_lora)),
                  pl.BlockSpec((bm, 128), lambda i: (i, off_kr // 128)),
                  pl.BlockSpec((bm, 128), rows), pl.BlockSpec((bm, 128), rows),
                  pl.BlockSpec((1, q_lora), fixed), pl.BlockSpec((1, kv_lora), fixed),
                  pl.BlockSpec((q_lora, hw), fixed), pl.BlockSpec((kv_lora, hw), fixed)],
        out_specs=(pl.BlockSpec((bm, hw), rows), pl.BlockSpec((bm, hw), rows),
                   pl.BlockSpec((bm, heads * V_HEAD), rows)),
        compiler_params=_params(("arbitrary",), vmem),
        name="mla_qkv",
    )(u, u, u, tq, tk, g_q.reshape(1, -1), g_kv.reshape(1, -1), wq, wkv)


def _flash_body(*refs, bk, n_blocks, aliased):
    q_ref, k_ref, vt_ref = refs[:3]
    o_ref = refs[4] if aliased else refs[3]
    s_refs, mb_refs, acc_ref, m_ref = refs[-6:-4], refs[-4:-2], refs[-2], refs[-1]
    q = q_ref[...]
    ones = jnp.ones((V_HEAD, bk), BF16)

    def scores(blk, buf):
        keys = k_ref[pl.ds(pl.multiple_of(blk * bk, bk), bk), :]
        s = lax.dot_general(keys, q, (((1,), (1,)), ((), ())), preferred_element_type=F32)
        s_refs[buf][...] = s
        mb_refs[buf][...] = jnp.max(s, axis=0, keepdims=True)

    def accumulate(blk, buf):
        m_old = m_ref[...]
        m_new = jnp.maximum(m_old, mb_refs[buf][...])
        p = jnp.exp2(s_refs[buf][...] - m_new).astype(BF16)
        vals = jnp.concatenate([vt_ref[blk], ones], axis=0)
        acc_ref[...] = jnp.exp2(m_old - m_new) * acc_ref[...] + jnp.dot(vals, p, preferred_element_type=F32)
        m_ref[...] = m_new

    m_ref[...] = jnp.full(m_ref.shape, -jnp.inf, F32)
    acc_ref[...] = jnp.zeros(acc_ref.shape, F32)
    scores(0, 0)
    n_pairs = (n_blocks - 1) // 2

    def pair(u, carry):
        accumulate(2 * u, 0)
        scores(2 * u + 1, 1)
        accumulate(2 * u + 1, 1)
        scores(2 * u + 2, 0)
        return carry

    lax.fori_loop(0, n_pairs, pair, 0)
    if (n_blocks - 1) % 2:
        accumulate(n_blocks - 2, 0)
        scores(n_blocks - 1, 1)
        accumulate(n_blocks - 1, 1)
    else:
        accumulate(n_blocks - 1, 0)
    acc = acc_ref[...]
    o_ref[...] = (acc[:V_HEAD, :] / acc[V_HEAD:, :]).T.astype(o_ref.dtype)


def attention(q, k, v, heads, *, q_rows, kv_row0, kv_rows, out=None):
    m = q.shape[0]
    aliased = out is not None
    q_row0 = kv_row0 if aliased else 0
    bq = _pick(q_rows, (2048, 512, 256, 128))
    bk = _pick(kv_rows, (1664, 1024, 512, 256, 128))
    n_blocks = kv_rows // bk
    assert q_row0 % bq == 0 and kv_row0 % kv_rows == 0
    qb0, kb0 = q_row0 // bq, kv_row0 // kv_rows
    vt = v[kv_row0:kv_row0 + kv_rows].reshape(n_blocks, bk, heads, V_HEAD).transpose(2, 0, 3, 1)
    once = pl.Buffered(1)
    in_specs = [pl.BlockSpec((bq, QK_HEAD), lambda h, i: (qb0 + i, h)),
                pl.BlockSpec((kv_rows, QK_HEAD), lambda h, i: (kb0, h), pipeline_mode=once),
                pl.BlockSpec((None, n_blocks, V_HEAD, bk), lambda h, i: (h, 0, 0, 0), pipeline_mode=once)]
    args = [q, k, vt]
    if aliased:
        in_specs.append(pl.BlockSpec(memory_space=pl.ANY))
        args.append(out)
    vmem = kv_rows * (QK_HEAD + V_HEAD) * 2 + 4 * bq * QK_HEAD * 2 + 2 * bq * bk * 4 + 2 * bq * min(bk, 1024) * 4
    return pl.pallas_call(
        functools.partial(_flash_body, bk=bk, n_blocks=n_blocks, aliased=aliased),
        out_shape=jax.ShapeDtypeStruct((m, heads * V_HEAD), BF16),
        grid=(heads, q_rows // bq),
        in_specs=in_specs,
        out_specs=pl.BlockSpec((bq, V_HEAD), lambda h, i: (qb0 + i, h)),
        scratch_shapes=[pltpu.VMEM((bk, bq), F32), pltpu.VMEM((bk, bq), F32),
                        pltpu.VMEM((1, bq), F32), pltpu.VMEM((1, bq), F32),
                        pltpu.VMEM((2 * V_HEAD, bq), F32), pltpu.VMEM((1, bq), F32)],
        input_output_aliases={3: 0} if aliased else {},
        compiler_params=_params(("arbitrary", "arbitrary"), vmem),
        name="attention_ctx" if aliased else "attention",
    )(*args)


def _seq_edges(tile, tiles_x, tiles_all):
    first = jnp.logical_or(tile == 0, tile == tiles_x)
    last = jnp.logical_or(tile == tiles_x - 1, tile == tiles_all - 1)
    return first, last


def _depthwise(ext_ref, w_ref, bm, taps, width, rc=128, cc=128):
    base = CONV_HALO - taps // 2
    rc = min(rc, bm)
    cc = min(cc, width)
    span = rc + 2 * CONV_HALO
    by_shift = {}
    for k in range(taps):
        by_shift.setdefault((base + k) % 8, []).append(k)
    for r0 in range(0, bm, rc):
        for c0 in range(0, width, cc):
            window = ext_ref[r0:r0 + span, c0:c0 + cc]
            acc = jnp.zeros((rc, cc), F32)
            for shift, ks in sorted(by_shift.items()):
                rolled = window if shift == 0 else pltpu.roll(window, span - shift, 0)
                for k in ks:
                    a0 = base + k - shift
                    acc = acc + rolled[a0:a0 + rc, :] * w_ref[k:k + 1, c0:c0 + cc]
            yield r0, rc, c0, cc, acc


def _conformer_body(a_ref, b_ref, ap_ref, bp_ref, an_ref, bn_ref, w_ref, cb_ref, g_ref, be_ref, o_ref,
                    ext_ref, cv_ref, *, bm, taps, tiles_x, tiles_all):
    first, last = _seq_edges(pl.program_id(0), tiles_x, tiles_all)
    width = a_ref.shape[1]
    glu = lambda a, b: a * jax.nn.sigmoid(b)
    ext_ref[CONV_HALO:CONV_HALO + bm, :] = glu(a_ref[...], b_ref[...])
    ext_ref[0:CONV_HALO, :] = jnp.where(first, 0.0, glu(ap_ref[...], bp_ref[...]))
    ext_ref[CONV_HALO + bm:, :] = jnp.where(last, 0.0, glu(an_ref[...], bn_ref[...]))
    for r0, rc, c0, cc, acc in _depthwise(ext_ref, w_ref, bm, taps, width):
        cv_ref[r0:r0 + rc, c0:c0 + cc] = acc + cb_ref[:, c0:c0 + cc]
    y = cv_ref[...]
    mu = jnp.mean(y, axis=-1, keepdims=True)
    yc = y - mu
    yn = yc * lax.rsqrt(jnp.mean(yc * yc, axis=-1, keepdims=True) + EPS) * g_ref[...] + be_ref[...]
    o_ref[...] = _silu(yn).astype(o_ref.dtype)


def conformer_branch(u, ch, conv_w, conv_b, ln_g, ln_b, n_x):
    m = u.shape[0]
    taps = conv_w.shape[0]
    bm = _pick(math.gcd(n_x, m - n_x), (256, 128))
    hb = bm // CONV_HALO
    n_hb = m // CONV_HALO
    tiles_all = m // bm
    prev = lambda i: (jnp.maximum(i * hb - 1, 0), 0)
    prev_b = lambda i: (jnp.maximum(i * hb - 1, 0), 1)
    nxt = lambda i: (jnp.minimum((i + 1) * hb, n_hb - 1), 0)
    nxt_b = lambda i: (jnp.minimum((i + 1) * hb, n_hb - 1), 1)
    row = lambda i: (0, 0)
    vmem = 2 * 2 * (bm + 2 * CONV_HALO) * ch * 4 + 2 * bm * ch * 2 + (2 * bm + 2 * CONV_HALO) * ch * 4 \
        + 4 * bm * ch * 4
    return pl.pallas_call(
        functools.partial(_conformer_body, bm=bm, taps=taps, tiles_x=n_x // bm, tiles_all=tiles_all),
        out_shape=jax.ShapeDtypeStruct((m, ch), BF16),
        grid=(tiles_all,),
        in_specs=[pl.BlockSpec((bm, ch), lambda i: (i, 0)), pl.BlockSpec((bm, ch), lambda i: (i, 1)),
                  pl.BlockSpec((CONV_HALO, ch), prev), pl.BlockSpec((CONV_HALO, ch), prev_b),
                  pl.BlockSpec((CONV_HALO, ch), nxt), pl.BlockSpec((CONV_HALO, ch), nxt_b),
                  pl.BlockSpec((taps, ch), row), pl.BlockSpec((1, ch), row),
                  pl.BlockSpec((1, ch), row), pl.BlockSpec((1, ch), row)],
        out_specs=pl.BlockSpec((bm, ch), lambda i: (i, 0)),
        scratch_shapes=[pltpu.VMEM((bm + 2 * CONV_HALO, ch), F32), pltpu.VMEM((bm, ch), F32)],
        compiler_params=_params(("arbitrary",), vmem),
        name="conformer_branch",
    )(u, u, u, u, u, u, conv_w, conv_b.reshape(1, ch), ln_g.reshape(1, ch), ln_b.reshape(1, ch))


def _ssd_conv_body(x_ref, xp_ref, xn_ref, w_ref, cb_ref, o_ref, ext_ref, *, bm, taps, tiles_x, tiles_all):
    first, last = _seq_edges(pl.program_id(1), tiles_x, tiles_all)
    width = x_ref.shape[1]
    ext_ref[CONV_HALO:CONV_HALO + bm, :] = x_ref[...]
    ext_ref[0:CONV_HALO, :] = jnp.where(first, 0.0, xp_ref[...])
    ext_ref[CONV_HALO + bm:, :] = jnp.where(last, 0.0, xn_ref[...])
    for r0, rc, c0, cc, acc in _depthwise(ext_ref, w_ref, bm, taps, width):
        o_ref[r0:r0 + rc, c0:c0 + cc] = _silu(acc + cb_ref[:, c0:c0 + cc]).astype(o_ref.dtype)


def ssd_conv(xbc, conv_w, conv_b, n_x):
    m, ch = xbc.shape
    taps = conv_w.shape[0]
    bm = _pick(math.gcd(n_x, m - n_x), (256, 128))
    bc = _pick(ch, (1024, 512, 256, 128))
    hb = bm // CONV_HALO
    n_hb = m // CONV_HALO
    tiles_all = m // bm
    vmem = 2 * (2 * bm + 2 * CONV_HALO) * bc * 4 + (bm + 2 * CONV_HALO) * bc * 4 + 2 * bm * bc * 4
    return pl.pallas_call(
        functools.partial(_ssd_conv_body, bm=bm, taps=taps, tiles_x=n_x // bm, tiles_all=tiles_all),
        out_shape=jax.ShapeDtypeStruct((m, ch), BF16),
        grid=(ch // bc, tiles_all),
        in_specs=[pl.BlockSpec((bm, bc), lambda c, i: (i, c)),
                  pl.BlockSpec((CONV_HALO, bc), lambda c, i: (jnp.maximum(i * hb - 1, 0), c)),
                  pl.BlockSpec((CONV_HALO, bc), lambda c, i: (jnp.minimum((i + 1) * hb, n_hb - 1), c)),
                  pl.BlockSpec((taps, bc), lambda c, i: (0, c)),
                  pl.BlockSpec((1, bc), lambda c, i: (0, c))],
        out_specs=pl.BlockSpec((bm, bc), lambda c, i: (i, c)),
        scratch_shapes=[pltpu.VMEM((bm + 2 * CONV_HALO, bc), F32)],
        compiler_params=_params(("arbitrary", "arbitrary"), vmem),
        name="ssd_conv",
    )(xbc, xbc, xbc, conv_w, conv_b.reshape(1, ch))


def _split3(v):
    hi = v.astype(BF16)
    r1 = v - hi.astype(F32)
    mid = r1.astype(BF16)
    lo = (r1 - mid.astype(F32)).astype(BF16)
    return hi, mid, lo


def _dt_body(raw_ref, bias_ref, alog_ref, dt_ref, ac_ref, *, n_heads):
    v = raw_ref[...] + bias_ref[...]
    dt = jnp.maximum(v, 0.0) + jnp.log1p(jnp.exp(-jnp.abs(v)))
    dt_ref[...] = dt
    dta = dt * (-jnp.exp(alog_ref[...]))
    ii = lax.broadcasted_iota(jnp.int32, (CHUNK, CHUNK), 0)
    jj = lax.broadcasted_iota(jnp.int32, (CHUNK, CHUNK), 1)
    lower = jnp.where(ii >= jj, 1.0, 0.0).astype(BF16)
    upper = jnp.where(ii <= jj, 1.0, 0.0).astype(BF16)
    pre = jnp.zeros(dta.shape, F32)
    suf = jnp.zeros(dta.shape, F32)
    for part in _split3(dta)[::-1]:
        pre = pre + jnp.dot(lower, part, preferred_element_type=F32)
        suf = suf + jnp.dot(upper, part, preferred_element_type=F32)
    col = lax.broadcasted_iota(jnp.int32, dta.shape, 1)
    ac_ref[...] = jnp.where(col < n_heads, pre, suf)


def ssd_dt(raw, dt_bias, a_log):
    m, w = raw.shape
    blk = pl.BlockSpec((CHUNK, w), lambda i: (i, 0))
    vec = pl.BlockSpec((1, w), lambda i: (0, 0))
    return pl.pallas_call(
        functools.partial(_dt_body, n_heads=w // 2),
        out_shape=(jax.ShapeDtypeStruct((m, w), F32), jax.ShapeDtypeStruct((m, w), F32)),
        grid=(m // CHUNK,),
        in_specs=[blk, vec, vec],
        out_specs=(blk, blk),
        compiler_params=_params(("arbitrary",), 16 * CHUNK * w * 4),
        name="ssd_dt",
    )(raw, dt_bias.reshape(1, w), a_log.reshape(1, w))


def _expand_cols(v_rows, sel, pieces=3):
    out = None
    for part in _split3(v_rows)[:pieces][::-1]:
        r = lax.dot_general(part, sel, (((0,), (0,)), ((), ())), preferred_element_type=F32)
        out = r if out is None else out + r
    return out


def _head_selector(hg, width_per_head):
    head = lax.broadcasted_iota(jnp.int32, (hg, hg * width_per_head), 0)
    col = lax.broadcasted_iota(jnp.int32, (hg, hg * width_per_head), 1)
    lo = head * width_per_head
    return jnp.where(col >= lo, jnp.where(col < lo + width_per_head, 1.0, 0.0), 0.0).astype(BF16)


def _ssd_body(x_ref, b_ref, c_ref, dt_ref, ac_ref, accol_ref, y_ref, st_ref, wx_ref, dec_ref, *, hg, p):
    d = pl.program_id(0)

    @pl.when(pl.program_id(2) == 0)
    def _():
        st_ref[...] = jnp.zeros(st_ref.shape, F32)

    b_t = b_ref[...].astype(F32).T.astype(BF16)
    cmat = c_ref[...].astype(BF16)
    cb = jnp.dot(cmat, b_t, preferred_element_type=F32)
    ii = lax.broadcasted_iota(jnp.int32, (CHUNK, CHUNK), 0)
    jj = lax.broadcasted_iota(jnp.int32, (CHUNK, CHUNK), 1)
    mask = jnp.where(d == 0, ii - jj, jj - ii) >= 0
    low_half = jj < p
    ac_rows = ac_ref[...]
    ac_cols = accol_ref[...]
    dt_x = _expand_cols(dt_ref[...], _head_selector(hg, p), pieces=2)
    dx = dt_x * x_ref[...].astype(F32)
    st = st_ref[...]
    y_state = jnp.dot(cmat, st.astype(BF16), preferred_element_type=F32)
    zero = jnp.zeros((CHUNK, CHUNK), BF16)
    for v in range(hg // 2):
        pair = slice(v * CHUNK, (v + 1) * CHUNK)
        heads = (2 * v, 2 * v + 1)
        wide = [jnp.broadcast_to(ac_cols[:, h:h + 1], (CHUNK, CHUNK)) for h in heads]
        ac_pair = jnp.where(low_half, wide[0], wide[1])
        last = jnp.where(d == 0, ac_pair[CHUNK - 1:CHUNK, :], ac_pair[0:1, :])
        mixes = [(cb * jnp.exp(jnp.where(mask, w - ac_rows[h:h + 1, :], -jnp.inf))).astype(BF16)
                 for w, h in zip(wide, heads)]
        dx_pair = dx[:, pair]
        dtx_pair = dx_pair.astype(BF16)
        rhs = jnp.concatenate([jnp.where(low_half, dtx_pair, zero), jnp.where(low_half, zero, dtx_pair)], axis=0)
        y_in = jnp.dot(jnp.concatenate(mixes, axis=1), rhs, preferred_element_type=F32)
        y_ref[:, pair] = (y_in + y_state[:, pair] * jnp.exp(ac_pair)).astype(y_ref.dtype)
        wx_ref[:, pair] = (jnp.exp(last - ac_pair) * dx_pair).astype(BF16)
        dec_ref[:, pair] = jnp.exp(last)
    st_ref[...] = st * dec_ref[...] + jnp.dot(b_t, wx_ref[...], preferred_element_type=F32)


def ssd_scan(xbc_c, dt_rows, ac_rows, ac_cols, d_inner, n_x):
    m = xbc_c.shape[0]
    groups = SSM_GROUPS
    gw = d_inner // groups
    hg = gw // SSM_HEADDIM
    assert SSM_HEADDIM * 2 == CHUNK and hg % 2 == 0
    n_xc = n_x // CHUNK
    n_all = m // CHUNK
    n_cc = n_all - n_xc
    b_blk0 = d_inner // D_STATE
    c_blk0 = b_blk0 + groups

    def chunk(d, t):
        fwd = jnp.where(t < n_cc, n_xc + t, t - n_cc)
        return jnp.where(d == 0, fwd, n_all - 1 - t)

    rows = pl.BlockSpec((None, None, None, hg, CHUNK), lambda d, g, t: (d, g, chunk(d, t), 0, 0))
    vmem = 4 * CHUNK * gw * 4 + 4 * CHUNK * D_STATE * 4 + D_STATE * gw * 4 + 12 * CHUNK * gw * 4 \
        + 2 * CHUNK * hg * CHUNK * 4
    return pl.pallas_call(
        functools.partial(_ssd_body, hg=hg, p=SSM_HEADDIM),
        out_shape=jax.ShapeDtypeStruct((2, m, d_inner), BF16),
        grid=(2, groups, n_all),
        in_specs=[pl.BlockSpec((CHUNK, gw), lambda d, g, t: (chunk(d, t), g)),
                  pl.BlockSpec((CHUNK, D_STATE), lambda d, g, t: (chunk(d, t), b_blk0 + g)),
                  pl.BlockSpec((CHUNK, D_STATE), lambda d, g, t: (chunk(d, t), c_blk0 + g)),
                  rows, rows,
                  pl.BlockSpec((None, None, CHUNK, hg), lambda d, g, t: (d, g, chunk(d, t), 0))],
        out_specs=pl.BlockSpec((None, CHUNK, gw), lambda d, g, t: (d, chunk(d, t), g)),
        scratch_shapes=[pltpu.VMEM((D_STATE, gw), F32), pltpu.VMEM((CHUNK, gw), BF16), pltpu.VMEM((1, gw), F32)],
        compiler_params=_params(("arbitrary", "arbitrary", "arbitrary"), vmem),
        name="ssd_scan",
    )(xbc_c, xbc_c, xbc_c, dt_rows, ac_rows, ac_cols)


def _gate_norm_body(yf_ref, yb_ref, xs_ref, z_ref, dsk_ref, g_ref, o_ref):
    y = yf_ref[...].astype(F32) + yb_ref[...].astype(F32) + dsk_ref[...] * xs_ref[...].astype(F32)
    y = y * _silu(z_ref[...].astype(F32))
    o_ref[...] = (y * lax.rsqrt(jnp.mean(y * y, axis=-1, keepdims=True) + EPS) * g_ref[...]).astype(o_ref.dtype)


def ssd_gate_norm(y2, xbc_c, z, d_skip_cols, norm_g, n_x):
    di = z.shape[1]
    bm = _pick(n_x, (128,))
    blk = pl.BlockSpec((bm, di), lambda i: (i, 0))
    vec = pl.BlockSpec((1, di), lambda i: (0, 0))
    return pl.pallas_call(
        _gate_norm_body,
        out_shape=jax.ShapeDtypeStruct((n_x, di), BF16),
        grid=(n_x // bm,),
        in_specs=[pl.BlockSpec((None, bm, di), lambda i: (0, i, 0)),
                  pl.BlockSpec((None, bm, di), lambda i: (1, i, 0)), blk, blk, vec, vec],
        out_specs=blk,
        compiler_params=_params(("arbitrary",), 2 * bm * di * 18 + 4 * bm * di * 4),
        name="ssd_gate_norm",
    )(y2, y2, xbc_c, z, d_skip_cols.reshape(1, di), norm_g.reshape(1, di))


def _rope_perm():
    d = jnp.arange(QK_ROPE)
    first = (d % (QK_ROPE // 2)) < QK_ROPE // 4
    return jnp.where(first, d + QK_ROPE // 4, d - QK_ROPE // 4), jnp.where(first, -1.0, 1.0).astype(F32)


def _rope_tables(n_x, n_ctx):
    rows = n_x // GRID_W
    row = jnp.repeat(jnp.arange(rows, dtype=F32), GRID_W)
    col = jnp.tile(jnp.arange(GRID_W, dtype=F32), rows)
    quarter = QK_ROPE // 4
    freqs = 1.0 / (ROPE_THETA ** (jnp.arange(quarter, dtype=F32) / quarter))
    ar = row[:, None] * freqs
    ac = col[:, None] * freqs
    ang = jnp.concatenate([ar, ar, ac, ac], axis=-1)
    tab = jnp.concatenate([jnp.cos(ang), jnp.sin(ang)], axis=-1)
    ctx_tab = jnp.concatenate([jnp.ones((n_ctx, QK_ROPE), F32), jnp.zeros((n_ctx, QK_ROPE), F32)], axis=-1)
    return jnp.concatenate([tab, ctx_tab], axis=0)


def _ffn(xa, mods, norm_g, w1, w3, w2, layer, n_x):
    h = norm_mod(xa, norm_g, mods, 3, 4, n_x)
    g = linear([h], [w1], w3_list=[w3], layer=layer, out_dtype=BF16, bn_opts=(256, 128), name="ffn_up")
    return linear([g], [w2[layer].astype(BF16)], out_dtype=F32, res=xa, mods=mods, gate_blk=5, n_x=n_x,
                  bm_opts=(640, 512, 256, 128), bn_opts=(256, 128), name="ffn_down")


def _attn_conv_layer(xa, mods, n_x, norm1_g, w_in, g_q, w_uq, g_kv, w_ukv, conv_w, conv_b, ln_g, ln_b, w_o):
    m = xa.shape[0]
    q_lora, kv_lora, ch = g_q.shape[0], g_kv.shape[0], conv_b.shape[0]
    heads = w_uq.shape[1] // (QK_NOPE + QK_ROPE)
    perm, sign = _rope_perm()
    c0 = q_lora + kv_lora + QK_ROPE
    w_kr = w_in[:, q_lora + kv_lora:c0]
    w_in_r = jnp.concatenate([w_in[:, c0:], w_in[:, :q_lora + kv_lora], w_kr, w_kr[:, perm] * sign],
                             axis=1).astype(BF16)
    off_cq, off_ckv, off_kr = 2 * ch, 2 * ch + q_lora, 2 * ch + q_lora + kv_lora
    wq3 = w_uq.reshape(q_lora, heads, QK_NOPE + QK_ROPE)
    wq_rope = wq3[:, :, QK_NOPE:]
    wq = jnp.concatenate([wq3, wq_rope[:, :, perm] * sign], axis=-1).reshape(q_lora, heads * QK_HEAD).astype(BF16)
    wkv = w_ukv.astype(BF16)
    tk = _rope_tables(n_x, m - n_x)
    tq = tk * Q_SCALE

    h = norm_mod(xa, norm1_g, mods, 0, 1, n_x)
    u = linear([h], [w_in_r], out_dtype=F32, bn_opts=(640, 512, 256, 128), name="attn_in_proj")
    q, k, v = mla_qkv(u, off_cq, q_lora, off_ckv, kv_lora, off_kr, tq, tk, g_q, g_kv, wq, wkv, heads)
    cv = conformer_branch(u, ch, conv_w, conv_b, ln_g, ln_b, n_x)
    o = attention(q, k, v, heads, q_rows=n_x, kv_row0=0, kv_rows=m)
    o = attention(q, k, v, heads, q_rows=m - n_x, kv_row0=n_x, kv_rows=m - n_x, out=o)
    hv = heads * V_HEAD
    return linear([o, cv], [w_o[:hv].astype(BF16), w_o[hv:].astype(BF16)], out_dtype=F32, res=xa, mods=mods,
                  gate_blk=2, n_x=n_x, name="attn_out_proj")


def _ssd_layer(xa, mods, n_x, norm1_g, w_in, conv_w, conv_b, dt_bias, a_log, d_skip, norm_g, w_out):
    m = xa.shape[0]
    d_inner = norm_g.shape[0]
    n_heads = d_skip.shape[0]
    groups = SSM_GROUPS
    hg = n_heads // groups
    xbc_w = conv_b.shape[0]
    h = norm_mod(xa, norm1_g, mods, 0, 1, n_x)
    z = linear([h], [w_in], n=d_inner, out_dtype=BF16, name="ssd_in_z")
    xbc = linear([h], [w_in], n=xbc_w, col0=d_inner, out_dtype=F32, name="ssd_in_xbc")
    dt_raw = linear([h], [w_in], n=2 * n_heads, col0=d_inner + xbc_w, out_dtype=F32, name="ssd_in_dt")
    xbc_c = ssd_conv(xbc, conv_w, conv_b, n_x)
    dt, ac = ssd_dt(dt_raw, dt_bias.reshape(-1), a_log.reshape(-1))
    to_rows = lambda t: t.reshape(m // CHUNK, CHUNK, 2, groups, hg).transpose(2, 3, 0, 4, 1)
    ac_cols = ac.reshape(m, 2, groups, hg).transpose(1, 2, 0, 3)
    y2 = ssd_scan(xbc_c, to_rows(dt), to_rows(ac), ac_cols, d_inner, n_x)
    yn = ssd_gate_norm(y2, xbc_c, z, jnp.repeat(d_skip, SSM_HEADDIM), norm_g, n_x)
    return linear([yn], [w_out.astype(BF16)], out_dtype=F32, res=xa, mods=mods, gate_blk=2,
                  bm_opts=(1024, 512, 256, 128), bn_opts=(256, 128), name="ssd_out_proj")


def kernel(x, c, ctx, c_ctx, ada_w, ada_b, norm1_g, norm2_g, a_w_in, a_g_q, a_w_uq, a_g_kv, a_w_ukv, a_conv_w,
           a_conv_b, a_ln_g, a_ln_b, a_w_o, m_w_in, m_conv_w, m_conv_b, m_dt_bias, m_a_log, m_d, m_norm_g,
           m_w_out, ffn_w1, ffn_w3, ffn_w2, final_g):
    assert x.shape[0] == 1 and ada_w.shape[0] == 2
    n_x, d = x.shape[1], x.shape[2]
    xa = jnp.concatenate([x[0], ctx[0]], axis=0)
    cond8 = jnp.zeros((8, d), F32).at[0].set(c[0]).at[1].set(c_ctx)
    mods = adaln(cond8, ada_w, ada_b)

    xa = _attn_conv_layer(xa, mods[0], n_x, norm1_g[0], a_w_in[0], a_g_q[0], a_w_uq[0], a_g_kv[0], a_w_ukv[0],
                          a_conv_w[0], a_conv_b[0], a_ln_g[0], a_ln_b[0], a_w_o[0])
    xa = _ffn(xa, mods[0], norm2_g[0], ffn_w1, ffn_w3, ffn_w2, 0, n_x)
    xl = _ssd_layer(xa, mods[1], n_x, norm1_g[1], m_w_in, m_conv_w[0], m_conv_b[0], m_dt_bias[0], m_a_log[0],
                    m_d[0], m_norm_g[0], m_w_out[0])
    xl = _ffn(xl, mods[1], norm2_g[1], ffn_w1, ffn_w3, ffn_w2, 1, xl.shape[0])
    return final_rmsnorm(xl, final_g)[None]
```

```python
import functools
import math

import jax
import jax.numpy as jnp
from jax import lax
from jax.experimental import pallas as pl
from jax.experimental.pallas import tpu as pltpu

F32 = jnp.float32
BF16 = jnp.bfloat16
EPS = 1e-6

GRID_W = 64
ROPE_THETA = 10000.0
QK_NOPE = 128
QK_ROPE = 64
V_HEAD = 128
QK_HEAD = QK_NOPE + 2 * QK_ROPE
ATTN_SCALE = 1.0 / math.sqrt(QK_NOPE + QK_ROPE)
Q_SCALE = ATTN_SCALE * math.log2(math.e)
SSM_GROUPS = 8
SSM_HEADDIM = 64
D_STATE = 128
CHUNK = 128
CONV_HALO = 16

V7X_VMEM_BYTES = 64 * 1024 * 1024
VMEM_CAP = V7X_VMEM_BYTES - 6 * 1024 * 1024


def _pick(n, candidates):
    for c in candidates:
        if c <= n and n % c == 0:
            return c
    return n


def _params(semantics, vmem_bytes, flags=None):
    limit = int(min(max(vmem_bytes * 1.25 + (4 << 20), 24 << 20), VMEM_CAP))
    return pltpu.CompilerParams(dimension_semantics=semantics, vmem_limit_bytes=limit, flags=flags)


def _silu(v):
    return v * jax.nn.sigmoid(v)


def _ctx_rows(tile, bm, n_x):
    rows = tile * bm + lax.broadcasted_iota(jnp.int32, (bm, 1), 0)
    return rows >= n_x


def _adaln_body(c_ref, w_ref, b_ref, o_ref):
    a = _silu(c_ref[...]).astype(BF16)
    o_ref[...] = jnp.dot(a, w_ref[...].astype(BF16), preferred_element_type=F32) + b_ref[...]


def adaln(cond8, ada_w, ada_b):
    depth, d, n = ada_w.shape
    tn = _pick(n, (512, 256, 128))
    return pl.pallas_call(
        _adaln_body,
        out_shape=jax.ShapeDtypeStruct((depth, 8, n), F32),
        grid=(depth, n // tn),
        in_specs=[pl.BlockSpec((8, d), lambda l, j: (0, 0)),
                  pl.BlockSpec((None, d, tn), lambda l, j: (l, 0, j)),
                  pl.BlockSpec((None, 1, tn), lambda l, j: (l, 0, j))],
        out_specs=pl.BlockSpec((None, 8, tn), lambda l, j: (l, 0, j)),
        compiler_params=_params(("arbitrary", "arbitrary"), 3 * d * tn * 4),
        name="adaln",
    )(cond8, ada_w, ada_b.reshape(depth, 1, n))


def _norm_mod_body(x_ref, g_ref, sh_ref, sc_ref, o_ref, *, bm, n_x):
    x = x_ref[...]
    y = x * lax.rsqrt(jnp.mean(x * x, axis=-1, keepdims=True) + EPS) * g_ref[...]
    is_c = _ctx_rows(pl.program_id(0), bm, n_x)
    sc = jnp.where(is_c, sc_ref[1:2, :], sc_ref[0:1, :])
    sh = jnp.where(is_c, sh_ref[1:2, :], sh_ref[0:1, :])
    o_ref[...] = (y * (1.0 + sc) + sh).astype(o_ref.dtype)


def norm_mod(x, g, mods, shift_blk, scale_blk, n_x):
    m, d = x.shape
    bm = _pick(m, (320, 256, 128))
    return pl.pallas_call(
        functools.partial(_norm_mod_body, bm=bm, n_x=n_x),
        out_shape=jax.ShapeDtypeStruct((m, d), BF16),
        grid=(m // bm,),
        in_specs=[pl.BlockSpec((bm, d), lambda i: (i, 0)),
                  pl.BlockSpec((1, d), lambda i: (0, 0)),
                  pl.BlockSpec((8, d), lambda i: (0, shift_blk)),
                  pl.BlockSpec((8, d), lambda i: (0, scale_blk))],
        out_specs=pl.BlockSpec((bm, d), lambda i: (i, 0)),
        compiler_params=_params(("arbitrary",), 2 * bm * d * 6 + 6 * bm * d * 4),
        name="norm_mod",
    )(x, g.reshape(1, d), mods, mods)


def _rmsnorm_body(x_ref, g_ref, o_ref):
    x = x_ref[...]
    o_ref[...] = x * lax.rsqrt(jnp.mean(x * x, axis=-1, keepdims=True) + EPS) * g_ref[...]


def final_rmsnorm(x, g):
    m, d = x.shape
    bm = _pick(m, (256, 128))
    return pl.pallas_call(
        _rmsnorm_body,
        out_shape=jax.ShapeDtypeStruct((m, d), F32),
        grid=(m // bm,),
        in_specs=[pl.BlockSpec((bm, d), lambda i: (i, 0)), pl.BlockSpec((1, d), lambda i: (0, 0))],
        out_specs=pl.BlockSpec((bm, d), lambda i: (i, 0)),
        compiler_params=_params(("arbitrary",), 6 * bm * d * 4),
        name="final_rmsnorm",
    )(x, g.reshape(1, d))


def _linear_body(*refs, n_pairs, swiglu, has_res, bm, n_x):
    a_refs = refs[:n_pairs]
    w_refs = refs[n_pairs:2 * n_pairs]
    pos = 2 * n_pairs
    if swiglu:
        w3_refs = refs[pos:pos + n_pairs]
        pos += n_pairs
    if has_res:
        res_ref, gate_ref = refs[pos], refs[pos + 1]
        pos += 2
    o_ref = refs[pos]

    def contract(ws):
        acc = None
        for a_ref, w_ref in zip(a_refs, ws):
            part = jnp.dot(a_ref[...], w_ref[...].astype(BF16), preferred_element_type=F32)
            acc = part if acc is None else acc + part
        return acc

    acc = contract(w_refs)
    if swiglu:
        acc = _silu(acc) * contract(w3_refs)
    if has_res:
        is_c = _ctx_rows(pl.program_id(0), bm, n_x)
        gate = jnp.where(is_c, gate_ref[1:2, :], gate_ref[0:1, :])
        acc = res_ref[...] + gate * acc
    o_ref[...] = acc.astype(o_ref.dtype)


def linear(a_list, w_list, *, out_dtype, w3_list=None, res=None, mods=None, gate_blk=None, n_x=None, n=None, col0=0,
           layer=0, bm_opts=(1280, 1024, 640, 512, 256, 128), bn_opts=(512, 640, 256, 128), name="linear"):
    m = a_list[0].shape[0]
    n = w_list[0].shape[-1] if n is None else n
    bm = _pick(m, bm_opts)
    bn = _pick(math.gcd(n, col0) if col0 else n, bn_opts)
    cb0 = col0 // bn
    swiglu = w3_list is not None
    has_res = res is not None
    n_pairs = len(a_list)
    in_specs, args, vmem = [], [], 0
    for a in a_list:
        k = a.shape[1]
        in_specs.append(pl.BlockSpec((bm, k), lambda i, j: (i, 0)))
        args.append(a)
        vmem += 2 * bm * k * a.dtype.itemsize
    for ws in ([w_list, w3_list] if swiglu else [w_list]):
        for w in ws:
            k = w.shape[-2]
            if w.ndim == 3:
                in_specs.append(pl.BlockSpec((None, k, bn), lambda i, j: (layer, 0, cb0 + j)))
            else:
                in_specs.append(pl.BlockSpec((k, bn), lambda i, j: (0, cb0 + j)))
            args.append(w)
            vmem += 2 * k * bn * w.dtype.itemsize + (k * bn * 2 if w.dtype == F32 else 0)
    if has_res:
        in_specs.append(pl.BlockSpec((bm, bn), lambda i, j: (i, j)))
        args.append(res)
        gb = gate_blk * (n // bn)
        in_specs.append(pl.BlockSpec((8, bn), lambda i, j: (0, gb + j)))
        args.append(mods)
        vmem += 2 * bm * bn * 4
    vmem += 2 * bm * bn * jnp.dtype(out_dtype).itemsize + 3 * bm * bn * 4
    return pl.pallas_call(
        functools.partial(_linear_body, n_pairs=n_pairs, swiglu=swiglu, has_res=has_res, bm=bm,
                          n_x=m if n_x is None else n_x),
        out_shape=jax.ShapeDtypeStruct((m, n), out_dtype),
        grid=(m // bm, n // bn),
        in_specs=in_specs,
        out_specs=pl.BlockSpec((bm, bn), lambda i, j: (i, j)),
        compiler_params=_params(("arbitrary", "arbitrary"), vmem),
        name=name,
    )(*args)


def _qkv_body(cq_ref, ckv_ref, kr_ref, tq_ref, tk_ref, gq_ref, gkv_ref, wq_ref, wkv_ref,
              q_ref, k_ref, v_ref, *, heads):
    def rms(v, g):
        return (v * lax.rsqrt(jnp.mean(v * v, axis=-1, keepdims=True) + EPS) * g).astype(BF16)

    nq = rms(cq_ref[...], gq_ref[...])
    nkv = rms(ckv_ref[...], gkv_ref[...])
    tq = tq_ref[...]
    kp = kr_ref[...] * tk_ref[...]
    kr2 = (kp + pltpu.roll(kp, QK_ROPE, 1)).astype(BF16)
    for h in range(heads):
        lo = h * QK_HEAD
        t = jnp.dot(nq, wq_ref[:, lo:lo + QK_HEAD], preferred_element_type=F32)
        q_ref[:, lo:lo + QK_NOPE] = (t[:, :QK_NOPE] * Q_SCALE).astype(BF16)
        q_ref[:, lo + QK_NOPE:lo + QK_HEAD] = (t[:, QK_NOPE:] * tq).astype(BF16)
        t = jnp.dot(nkv, wkv_ref[:, lo:lo + QK_HEAD], preferred_element_type=F32)
        k_ref[:, lo:lo + QK_NOPE] = t[:, :QK_NOPE].astype(BF16)
        k_ref[:, lo + QK_NOPE:lo + QK_HEAD] = kr2
        v_ref[:, h * V_HEAD:(h + 1) * V_HEAD] = t[:, QK_NOPE:].astype(BF16)


def mla_qkv(u, off_cq, q_lora, off_ckv, kv_lora, off_kr, tq, tk, g_q, g_kv, wq, wkv, heads):
    m = u.shape[0]
    bm = _pick(m, (256, 128))
    rows = lambda i: (i, 0)
    fixed = lambda i: (0, 0)
    hw = heads * QK_HEAD
    vmem = 2 * bm * (q_lora + kv_lora + 3 * 128) * 4 + 2 * (q_lora + kv_lora) * hw * 2 \
        + 2 * bm * (2 * hw + heads * V_HEAD) * 2 + 4 * bm * hw * 4
    return pl.pallas_call(
        functools.partial(_qkv_body, heads=heads),
        out_shape=(jax.ShapeDtypeStruct((m, hw), BF16), jax.ShapeDtypeStruct((m, hw), BF16),
                   jax.ShapeDtypeStruct((m, heads * V_HEAD), BF16)),
        grid=(m // bm,),
        in_specs=[pl.BlockSpec((bm, q_lora), lambda i: (i, off_cq // q_lora)),
                  pl.BlockSpec((bm, kv_lora), lambda i: (i, off_ckv // kv_lora)),
                  pl.BlockSpec((bm, 128), lambda i: (i, off_kr // 128)),
                  pl.BlockSpec((bm, 128), rows), pl.BlockSpec((bm, 128), rows),
                  pl.BlockSpec((1, q_lora), fixed), pl.BlockSpec((1, kv_lora), fixed),
                  pl.BlockSpec((q_lora, hw), fixed), pl.BlockSpec((kv_lora, hw), fixed)],
        out_specs=(pl.BlockSpec((bm, hw), rows), pl.BlockSpec((bm, hw), rows),
                   pl.BlockSpec((bm, heads * V_HEAD), rows)),
        compiler_params=_params(("arbitrary",), vmem),
        name="mla_qkv",
    )(u, u, u, tq, tk, g_q.reshape(1, -1), g_kv.reshape(1, -1), wq, wkv)


def _flash_body(q_ref, k_ref, vt_ref, o_ref, *scratch, bk, n_blocks):
    s_refs, mb_refs, acc_ref, m_ref = scratch[0:2], scratch[2:4], scratch[4], scratch[5]
    q = q_ref[...]
    ones = jnp.ones((V_HEAD, bk), BF16)

    def scores(blk, buf):
        keys = k_ref[pl.ds(pl.multiple_of(blk * bk, bk), bk), :]
        s = lax.dot_general(keys, q, (((1,), (1,)), ((), ())), preferred_element_type=F32)
        s_refs[buf][...] = s
        mb_refs[buf][...] = jnp.max(s, axis=0, keepdims=True)

    def accumulate(blk, buf):
        m_old = m_ref[...]
        m_new = jnp.maximum(m_old, mb_refs[buf][...])
        p = jnp.exp2(s_refs[buf][...] - m_new).astype(BF16)
        vals = jnp.concatenate([vt_ref[blk], ones], axis=0)
        acc_ref[...] = jnp.exp2(m_old - m_new) * acc_ref[...] + jnp.dot(vals, p, preferred_element_type=F32)
        m_ref[...] = m_new

    m_ref[...] = jnp.full(m_ref.shape, -jnp.inf, F32)
    acc_ref[...] = jnp.zeros(acc_ref.shape, F32)
    scores(0, 0)
    n_pairs = (n_blocks - 1) // 2

    def pair(u, carry):
        accumulate(2 * u, 0)
        scores(2 * u + 1, 1)
        accumulate(2 * u + 1, 1)
        scores(2 * u + 2, 0)
        return carry

    lax.fori_loop(0, n_pairs, pair, 0)
    if (n_blocks - 1) % 2:
        accumulate(n_blocks - 2, 0)
        scores(n_blocks - 1, 1)
        accumulate(n_blocks - 1, 1)
    else:
        accumulate(n_blocks - 1, 0)
    acc = acc_ref[...]
    o_ref[...] = (acc[:V_HEAD, :] / acc[V_HEAD:, :]).T.astype(o_ref.dtype)


def attention(q, k, v, heads, *, q_row0, q_rows, kv_row0, kv_rows, name):
    bq = _pick(q_rows, (2048, 512, 256, 128))
    bk = _pick(kv_rows, (1664, 1024, 512, 256, 128))
    n_blocks = kv_rows // bk
    assert q_row0 % bq == 0 and kv_row0 % kv_rows == 0
    qb0, kb0 = q_row0 // bq, kv_row0 // kv_rows
    vt = v[kv_row0:kv_row0 + kv_rows].reshape(n_blocks, bk, heads, V_HEAD).transpose(2, 0, 3, 1)
    once = pl.Buffered(1)
    in_specs = [pl.BlockSpec((bq, QK_HEAD), lambda h, i: (qb0 + i, h)),
                pl.BlockSpec((kv_rows, QK_HEAD), lambda h, i: (kb0, h), pipeline_mode=once),
                pl.BlockSpec((None, n_blocks, V_HEAD, bk), lambda h, i: (h, 0, 0, 0), pipeline_mode=once)]
    vmem = kv_rows * (QK_HEAD + V_HEAD) * 2 + 4 * bq * QK_HEAD * 2 + 2 * bq * bk * 4 + 2 * bq * min(bk, 1024) * 4
    return pl.pallas_call(
        functools.partial(_flash_body, bk=bk, n_blocks=n_blocks),
        out_shape=jax.ShapeDtypeStruct((q_rows, heads * V_HEAD), BF16),
        grid=(heads, q_rows // bq),
        in_specs=in_specs,
        out_specs=pl.BlockSpec((bq, V_HEAD), lambda h, i: (i, h)),
        scratch_shapes=[pltpu.VMEM((bk, bq), F32), pltpu.VMEM((bk, bq), F32),
                        pltpu.VMEM((1, bq), F32), pltpu.VMEM((1, bq), F32),
                        pltpu.VMEM((2 * V_HEAD, bq), F32), pltpu.VMEM((1, bq), F32)],
        compiler_params=_params(("arbitrary", "arbitrary"), vmem),
        name=name,
    )(q, k, vt)


def _seq_edges(tile, tiles_x, tiles_all):
    first = jnp.logical_or(tile == 0, tile == tiles_x)
    last = jnp.logical_or(tile == tiles_x - 1, tile == tiles_all - 1)
    return first, last


def _depthwise(ext_ref, w_ref, bm, taps, width, rc=128, cc=128):
    base = CONV_HALO - taps // 2
    rc = min(rc, bm)
    cc = min(cc, width)
    span = rc + 2 * CONV_HALO
    by_shift = {}
    for k in range(taps):
        by_shift.setdefault((base + k) % 8, []).append(k)
    for r0 in range(0, bm, rc):
        for c0 in range(0, width, cc):
            window = ext_ref[r0:r0 + span, c0:c0 + cc]
            acc = jnp.zeros((rc, cc), F32)
            for shift, ks in sorted(by_shift.items()):
                rolled = window if shift == 0 else pltpu.roll(window, span - shift, 0)
                for k in ks:
                    a0 = base + k - shift
                    acc = acc + rolled[a0:a0 + rc, :] * w_ref[k:k + 1, c0:c0 + cc]
            yield r0, rc, c0, cc, acc


def _conformer_body(a_ref, b_ref, ap_ref, bp_ref, an_ref, bn_ref, w_ref, cb_ref, g_ref, be_ref, o_ref,
                    ext_ref, cv_ref, *, bm, taps, tiles_x, tiles_all):
    first, last = _seq_edges(pl.program_id(0), tiles_x, tiles_all)
    width = a_ref.shape[1]
    glu = lambda a, b: a * jax.nn.sigmoid(b)
    ext_ref[CONV_HALO:CONV_HALO + bm, :] = glu(a_ref[...], b_ref[...])
    ext_ref[0:CONV_HALO, :] = jnp.where(first, 0.0, glu(ap_ref[...], bp_ref[...]))
    ext_ref[CONV_HALO + bm:, :] = jnp.where(last, 0.0, glu(an_ref[...], bn_ref[...]))
    for r0, rc, c0, cc, acc in _depthwise(ext_ref, w_ref, bm, taps, width):
        cv_ref[r0:r0 + rc, c0:c0 + cc] = acc + cb_ref[:, c0:c0 + cc]
    y = cv_ref[...]
    mu = jnp.mean(y, axis=-1, keepdims=True)
    yc = y - mu
    yn = yc * lax.rsqrt(jnp.mean(yc * yc, axis=-1, keepdims=True) + EPS) * g_ref[...] + be_ref[...]
    o_ref[...] = _silu(yn).astype(o_ref.dtype)


def conformer_branch(u, ch, conv_w, conv_b, ln_g, ln_b, n_x):
    m = u.shape[0]
    taps = conv_w.shape[0]
    bm = _pick(math.gcd(n_x, m - n_x), (256, 128))
    hb = bm // CONV_HALO
    n_hb = m // CONV_HALO
    tiles_all = m // bm
    prev = lambda i: (jnp.maximum(i * hb - 1, 0), 0)
    prev_b = lambda i: (jnp.maximum(i * hb - 1, 0), 1)
    nxt = lambda i: (jnp.minimum((i + 1) * hb, n_hb - 1), 0)
    nxt_b = lambda i: (jnp.minimum((i + 1) * hb, n_hb - 1), 1)
    row = lambda i: (0, 0)
    vmem = 2 * 2 * (bm + 2 * CONV_HALO) * ch * 4 + 2 * bm * ch * 2 + (2 * bm + 2 * CONV_HALO) * ch * 4 \
        + 4 * bm * ch * 4
    return pl.pallas_call(
        functools.partial(_conformer_body, bm=bm, taps=taps, tiles_x=n_x // bm, tiles_all=tiles_all),
        out_shape=jax.ShapeDtypeStruct((m, ch), BF16),
        grid=(tiles_all,),
        in_specs=[pl.BlockSpec((bm, ch), lambda i: (i, 0)), pl.BlockSpec((bm, ch), lambda i: (i, 1)),
                  pl.BlockSpec((CONV_HALO, ch), prev), pl.BlockSpec((CONV_HALO, ch), prev_b),
                  pl.BlockSpec((CONV_HALO, ch), nxt), pl.BlockSpec((CONV_HALO, ch), nxt_b),
                  pl.BlockSpec((taps, ch), row), pl.BlockSpec((1, ch), row),
                  pl.BlockSpec((1, ch), row), pl.BlockSpec((1, ch), row)],
        out_specs=pl.BlockSpec((bm, ch), lambda i: (i, 0)),
        scratch_shapes=[pltpu.VMEM((bm + 2 * CONV_HALO, ch), F32), pltpu.VMEM((bm, ch), F32)],
        compiler_params=_params(("arbitrary",), vmem),
        name="conformer_branch",
    )(u, u, u, u, u, u, conv_w, conv_b.reshape(1, ch), ln_g.reshape(1, ch), ln_b.reshape(1, ch))


def _ssd_conv_body(x_ref, xp_ref, xn_ref, w_ref, cb_ref, o_ref, ext_ref, *, bm, taps, tiles_x, tiles_all):
    first, last = _seq_edges(pl.program_id(1), tiles_x, tiles_all)
    width = x_ref.shape[1]
    ext_ref[CONV_HALO:CONV_HALO + bm, :] = x_ref[...]
    ext_ref[0:CONV_HALO, :] = jnp.where(first, 0.0, xp_ref[...])
    ext_ref[CONV_HALO + bm:, :] = jnp.where(last, 0.0, xn_ref[...])
    for r0, rc, c0, cc, acc in _depthwise(ext_ref, w_ref, bm, taps, width):
        o_ref[r0:r0 + rc, c0:c0 + cc] = _silu(acc + cb_ref[:, c0:c0 + cc]).astype(o_ref.dtype)


def ssd_conv(xbc, conv_w, conv_b, n_x):
    m, ch = xbc.shape
    taps = conv_w.shape[0]
    bm = _pick(math.gcd(n_x, m - n_x), (256, 128))
    bc = _pick(ch, (1024, 512, 256, 128))
    hb = bm // CONV_HALO
    n_hb = m // CONV_HALO
    tiles_all = m // bm
    vmem = 2 * (2 * bm + 2 * CONV_HALO) * bc * 4 + (bm + 2 * CONV_HALO) * bc * 4 + 2 * bm * bc * 4
    return pl.pallas_call(
        functools.partial(_ssd_conv_body, bm=bm, taps=taps, tiles_x=n_x // bm, tiles_all=tiles_all),
        out_shape=jax.ShapeDtypeStruct((m, ch), BF16),
        grid=(ch // bc, tiles_all),
        in_specs=[pl.BlockSpec((bm, bc), lambda c, i: (i, c)),
                  pl.BlockSpec((CONV_HALO, bc), lambda c, i: (jnp.maximum(i * hb - 1, 0), c)),
                  pl.BlockSpec((CONV_HALO, bc), lambda c, i: (jnp.minimum((i + 1) * hb, n_hb - 1), c)),
                  pl.BlockSpec((taps, bc), lambda c, i: (0, c)),
                  pl.BlockSpec((1, bc), lambda c, i: (0, c))],
        out_specs=pl.BlockSpec((bm, bc), lambda c, i: (i, c)),
        scratch_shapes=[pltpu.VMEM((bm + 2 * CONV_HALO, bc), F32)],
        compiler_params=_params(("arbitrary", "arbitrary"), vmem),
        name="ssd_conv",
    )(xbc, xbc, xbc, conv_w, conv_b.reshape(1, ch))


def _split3(v):
    hi = v.astype(BF16)
    r1 = v - hi.astype(F32)
    mid = r1.astype(BF16)
    lo = (r1 - mid.astype(F32)).astype(BF16)
    return hi, mid, lo


def _dt_body(raw_ref, bias_ref, alog_ref, dt_ref, ac_ref, *, n_heads):
    v = raw_ref[...] + bias_ref[...]
    dt = jnp.maximum(v, 0.0) + jnp.log1p(jnp.exp(-jnp.abs(v)))
    dt_ref[...] = dt
    dta = dt * (-jnp.exp(alog_ref[...]))
    ii = lax.broadcasted_iota(jnp.int32, (CHUNK, CHUNK), 0)
    jj = lax.broadcasted_iota(jnp.int32, (CHUNK, CHUNK), 1)
    lower = jnp.where(ii >= jj, 1.0, 0.0).astype(BF16)
    upper = jnp.where(ii <= jj, 1.0, 0.0).astype(BF16)
    pre = jnp.zeros(dta.shape, F32)
    suf = jnp.zeros(dta.shape, F32)
    for part in _split3(dta)[::-1]:
        pre = pre + jnp.dot(lower, part, preferred_element_type=F32)
        suf = suf + jnp.dot(upper, part, preferred_element_type=F32)
    col = lax.broadcasted_iota(jnp.int32, dta.shape, 1)
    ac_ref[...] = jnp.where(col < n_heads, pre, suf)


def ssd_dt(raw, dt_bias, a_log):
    m, w = raw.shape
    blk = pl.BlockSpec((CHUNK, w), lambda i: (i, 0))
    vec = pl.BlockSpec((1, w), lambda i: (0, 0))
    return pl.pallas_call(
        functools.partial(_dt_body, n_heads=w // 2),
        out_shape=(jax.ShapeDtypeStruct((m, w), F32), jax.ShapeDtypeStruct((m, w), F32)),
        grid=(m // CHUNK,),
        in_specs=[blk, vec, vec],
        out_specs=(blk, blk),
        compiler_params=_params(("arbitrary",), 16 * CHUNK * w * 4),
        name="ssd_dt",
    )(raw, dt_bias.reshape(1, w), a_log.reshape(1, w))


def _expand_cols(v_rows, sel, pieces=3):
    out = None
    for part in _split3(v_rows)[:pieces][::-1]:
        r = lax.dot_general(part, sel, (((0,), (0,)), ((), ())), preferred_element_type=F32)
        out = r if out is None else out + r
    return out


def _head_selector(hg, width_per_head):
    head = lax.broadcasted_iota(jnp.int32, (hg, hg * width_per_head), 0)
    col = lax.broadcasted_iota(jnp.int32, (hg, hg * width_per_head), 1)
    lo = head * width_per_head
    return jnp.where(col >= lo, jnp.where(col < lo + width_per_head, 1.0, 0.0), 0.0).astype(BF16)


def _ssd_body(x_ref, b_ref, c_ref, dt_ref, ac_ref, accol_ref, y_ref, st_ref, wx_ref, dec_ref, *, hg, p):
    d = pl.program_id(0)

    @pl.when(pl.program_id(2) == 0)
    def _():
        st_ref[...] = jnp.zeros(st_ref.shape, F32)

    b_t = b_ref[...].astype(F32).T.astype(BF16)
    cmat = c_ref[...].astype(BF16)
    cb = jnp.dot(cmat, b_t, preferred_element_type=F32)
    ii = lax.broadcasted_iota(jnp.int32, (CHUNK, CHUNK), 0)
    jj = lax.broadcasted_iota(jnp.int32, (CHUNK, CHUNK), 1)
    mask = jnp.where(d == 0, ii - jj, jj - ii) >= 0
    low_half = jj < p
    ac_rows = ac_ref[...]
    ac_cols = accol_ref[...]
    dt_x = _expand_cols(dt_ref[...], _head_selector(hg, p), pieces=2)
    dx = dt_x * x_ref[...].astype(F32)
    st = st_ref[...]
    y_state = jnp.dot(cmat, st.astype(BF16), preferred_element_type=F32)
    zero = jnp.zeros((CHUNK, CHUNK), BF16)
    for v in range(hg // 2):
        pair = slice(v * CHUNK, (v + 1) * CHUNK)
        heads = (2 * v, 2 * v + 1)
        wide = [jnp.broadcast_to(ac_cols[:, h:h + 1], (CHUNK, CHUNK)) for h in heads]
        ac_pair = jnp.where(low_half, wide[0], wide[1])
        last = jnp.where(d == 0, ac_pair[CHUNK - 1:CHUNK, :], ac_pair[0:1, :])
        mixes = [(cb * jnp.exp(jnp.where(mask, w - ac_rows[h:h + 1, :], -jnp.inf))).astype(BF16)
                 for w, h in zip(wide, heads)]
        dx_pair = dx[:, pair]
        dtx_pair = dx_pair.astype(BF16)
        rhs = jnp.concatenate([jnp.where(low_half, dtx_pair, zero), jnp.where(low_half, zero, dtx_pair)], axis=0)
        y_in = jnp.dot(jnp.concatenate(mixes, axis=1), rhs, preferred_element_type=F32)
        y_ref[:, pair] = (y_in + y_state[:, pair] * jnp.exp(ac_pair)).astype(y_ref.dtype)
        wx_ref[:, pair] = (jnp.exp(last - ac_pair) * dx_pair).astype(BF16)
        dec_ref[:, pair] = jnp.exp(last)
    st_ref[...] = st * dec_ref[...] + jnp.dot(b_t, wx_ref[...], preferred_element_type=F32)


def ssd_scan(xbc_c, dt_rows, ac_rows, ac_cols, d_inner, n_x):
    m = xbc_c.shape[0]
    groups = SSM_GROUPS
    gw = d_inner // groups
    hg = gw // SSM_HEADDIM
    assert SSM_HEADDIM * 2 == CHUNK and hg % 2 == 0
    n_xc = n_x // CHUNK
    n_all = m // CHUNK
    n_cc = n_all - n_xc
    b_blk0 = d_inner // D_STATE
    c_blk0 = b_blk0 + groups

    def chunk(d, t):
        fwd = jnp.where(t < n_cc, n_xc + t, t - n_cc)
        return jnp.where(d == 0, fwd, n_all - 1 - t)

    rows = pl.BlockSpec((None, None, None, hg, CHUNK), lambda d, g, t: (d, g, chunk(d, t), 0, 0))
    vmem = 4 * CHUNK * gw * 4 + 4 * CHUNK * D_STATE * 4 + D_STATE * gw * 4 + 12 * CHUNK * gw * 4 \
        + 2 * CHUNK * hg * CHUNK * 4
    return pl.pallas_call(
        functools.partial(_ssd_body, hg=hg, p=SSM_HEADDIM),
        out_shape=jax.ShapeDtypeStruct((2, m, d_inner), BF16),
        grid=(2, groups, n_all),
        in_specs=[pl.BlockSpec((CHUNK, gw), lambda d, g, t: (chunk(d, t), g)),
                  pl.BlockSpec((CHUNK, D_STATE), lambda d, g, t: (chunk(d, t), b_blk0 + g)),
                  pl.BlockSpec((CHUNK, D_STATE), lambda d, g, t: (chunk(d, t), c_blk0 + g)),
                  rows, rows,
                  pl.BlockSpec((None, None, CHUNK, hg), lambda d, g, t: (d, g, chunk(d, t), 0))],
        out_specs=pl.BlockSpec((None, CHUNK, gw), lambda d, g, t: (d, chunk(d, t), g)),
        scratch_shapes=[pltpu.VMEM((D_STATE, gw), F32), pltpu.VMEM((CHUNK, gw), BF16), pltpu.VMEM((1, gw), F32)],
        compiler_params=_params(("arbitrary", "arbitrary", "arbitrary"), vmem),
        name="ssd_scan",
    )(xbc_c, xbc_c, xbc_c, dt_rows, ac_rows, ac_cols)


def _gate_norm_body(yf_ref, yb_ref, xs_ref, z_ref, dsk_ref, g_ref, o_ref):
    y = yf_ref[...].astype(F32) + yb_ref[...].astype(F32) + dsk_ref[...] * xs_ref[...].astype(F32)
    y = y * _silu(z_ref[...].astype(F32))
    o_ref[...] = (y * lax.rsqrt(jnp.mean(y * y, axis=-1, keepdims=True) + EPS) * g_ref[...]).astype(o_ref.dtype)


def ssd_gate_norm(y2, xbc_c, z, d_skip_cols, norm_g, n_x):
    di = z.shape[1]
    bm = _pick(n_x, (128,))
    blk = pl.BlockSpec((bm, di), lambda i: (i, 0))
    vec = pl.BlockSpec((1, di), lambda i: (0, 0))
    return pl.pallas_call(
        _gate_norm_body,
        out_shape=jax.ShapeDtypeStruct((n_x, di), BF16),
        grid=(n_x // bm,),
        in_specs=[pl.BlockSpec((None, bm, di), lambda i: (0, i, 0)),
                  pl.BlockSpec((None, bm, di), lambda i: (1, i, 0)), blk, blk, vec, vec],
        out_specs=blk,
        compiler_params=_params(("arbitrary",), 2 * bm * di * 18 + 4 * bm * di * 4),
        name="ssd_gate_norm",
    )(y2, y2, xbc_c, z, d_skip_cols.reshape(1, di), norm_g.reshape(1, di))


def _rope_perm():
    d = jnp.arange(QK_ROPE)
    first = (d % (QK_ROPE // 2)) < QK_ROPE // 4
    return jnp.where(first, d + QK_ROPE // 4, d - QK_ROPE // 4), jnp.where(first, -1.0, 1.0).astype(F32)


def _rope_tables(n_x, n_ctx):
    rows = n_x // GRID_W
    row = jnp.repeat(jnp.arange(rows, dtype=F32), GRID_W)
    col = jnp.tile(jnp.arange(GRID_W, dtype=F32), rows)
    quarter = QK_ROPE // 4
    freqs = 1.0 / (ROPE_THETA ** (jnp.arange(quarter, dtype=F32) / quarter))
    ar = row[:, None] * freqs
    ac = col[:, None] * freqs
    ang = jnp.concatenate([ar, ar, ac, ac], axis=-1)
    tab = jnp.concatenate([jnp.cos(ang), jnp.sin(ang)], axis=-1)
    ctx_tab = jnp.concatenate([jnp.ones((n_ctx, QK_ROPE), F32), jnp.zeros((n_ctx, QK_ROPE), F32)], axis=-1)
    return jnp.concatenate([tab, ctx_tab], axis=0)


def _ffn(xa, mods, norm_g, w1, w3, w2, layer, n_x):
    h = norm_mod(xa, norm_g, mods, 3, 4, n_x)
    g = linear([h], [w1], w3_list=[w3], layer=layer, out_dtype=BF16, bn_opts=(256, 128), name="ffn_up")
    return linear([g], [w2[layer].astype(BF16)], out_dtype=F32, res=xa, mods=mods, gate_blk=5, n_x=n_x,
                  bm_opts=(640, 512, 256, 128), bn_opts=(256, 128), name="ffn_down")


def _attn_conv_layer(xa, mods, n_x, norm1_g, w_in, g_q, w_uq, g_kv, w_ukv, conv_w, conv_b, ln_g, ln_b, w_o):
    m = xa.shape[0]
    q_lora, kv_lora, ch = g_q.shape[0], g_kv.shape[0], conv_b.shape[0]
    heads = w_uq.shape[1] // (QK_NOPE + QK_ROPE)
    perm, sign = _rope_perm()
    c0 = q_lora + kv_lora + QK_ROPE
    w_kr = w_in[:, q_lora + kv_lora:c0]
    w_in_r = jnp.concatenate([w_in[:, c0:], w_in[:, :q_lora + kv_lora], w_kr, w_kr[:, perm] * sign],
                             axis=1).astype(BF16)
    off_cq, off_ckv, off_kr = 2 * ch, 2 * ch + q_lora, 2 * ch + q_lora + kv_lora
    wq3 = w_uq.reshape(q_lora, heads, QK_NOPE + QK_ROPE)
    wq_rope = wq3[:, :, QK_NOPE:]
    wq = jnp.concatenate([wq3, wq_rope[:, :, perm] * sign], axis=-1).reshape(q_lora, heads * QK_HEAD).astype(BF16)
    wkv = w_ukv.astype(BF16)
    tk = _rope_tables(n_x, m - n_x)
    tq = tk * Q_SCALE

    h = norm_mod(xa, norm1_g, mods, 0, 1, n_x)
    u = linear([h], [w_in_r], out_dtype=F32, bn_opts=(640, 512, 256, 128), name="attn_in_proj")
    q, k, v = mla_qkv(u, off_cq, q_lora, off_ckv, kv_lora, off_kr, tq, tk, g_q, g_kv, wq, wkv, heads)
    cv = conformer_branch(u, ch, conv_w, conv_b, ln_g, ln_b, n_x)
    o_x = attention(q, k, v, heads, q_row0=0, q_rows=n_x, kv_row0=0, kv_rows=m, name="attention")
    o_c = attention(q, k, v, heads, q_row0=n_x, q_rows=m - n_x, kv_row0=n_x, kv_rows=m - n_x, name="attention_ctx")
    o = jnp.concatenate([o_x, o_c], axis=0)
    hv = heads * V_HEAD
    return linear([o, cv], [w_o[:hv].astype(BF16), w_o[hv:].astype(BF16)], out_dtype=F32, res=xa, mods=mods,
                  gate_blk=2, n_x=n_x, name="attn_out_proj")


def _ssd_layer(xa, mods, n_x, norm1_g, w_in, conv_w, conv_b, dt_bias, a_log, d_skip, norm_g, w_out):
    m = xa.shape[0]
    d_inner = norm_g.shape[0]
    n_heads = d_skip.shape[0]
    groups = SSM_GROUPS
    hg = n_heads // groups
    xbc_w = conv_b.shape[0]
    h = norm_mod(xa, norm1_g, mods, 0, 1, n_x)
    z = linear([h], [w_in], n=d_inner, out_dtype=BF16, name="ssd_in_z")
    xbc = linear([h], [w_in], n=xbc_w, col0=d_inner, out_dtype=F32, name="ssd_in_xbc")
    dt_raw = linear([h], [w_in], n=2 * n_heads, col0=d_inner + xbc_w, out_dtype=F32, name="ssd_in_dt")
    xbc_c = ssd_conv(xbc, conv_w, conv_b, n_x)
    dt, ac = ssd_dt(dt_raw, dt_bias.reshape(-1), a_log.reshape(-1))
    to_rows = lambda t: t.reshape(m // CHUNK, CHUNK, 2, groups, hg).transpose(2, 3, 0, 4, 1)
    ac_cols = ac.reshape(m, 2, groups, hg).transpose(1, 2, 0, 3)
    y2 = ssd_scan(xbc_c, to_rows(dt), to_rows(ac), ac_cols, d_inner, n_x)
    yn = ssd_gate_norm(y2, xbc_c, z, jnp.repeat(d_skip, SSM_HEADDIM), norm_g, n_x)
    return linear([yn], [w_out.astype(BF16)], out_dtype=F32, res=xa, mods=mods, gate_blk=2,
                  bm_opts=(1024, 512, 256, 128), bn_opts=(256, 128), name="ssd_out_proj")


def kernel(x, c, ctx, c_ctx, ada_w, ada_b, norm1_g, norm2_g, a_w_in, a_g_q, a_w_uq, a_g_kv, a_w_ukv, a_conv_w,
           a_conv_b, a_ln_g, a_ln_b, a_w_o, m_w_in, m_conv_w, m_conv_b, m_dt_bias, m_a_log, m_d, m_norm_g,
           m_w_out, ffn_w1, ffn_w3, ffn_w2, final_g):
    assert x.shape[0] == 1 and ada_w.shape[0] == 2
    n_x, d = x.shape[1], x.shape[2]
    xa = jnp.concatenate([x[0], ctx[0]], axis=0)
    cond8 = jnp.zeros((8, d), F32).at[0].set(c[0]).at[1].set(c_ctx)
    mods = adaln(cond8, ada_w, ada_b)

    xa = _attn_conv_layer(xa, mods[0], n_x, norm1_g[0], a_w_in[0], a_g_q[0], a_w_uq[0], a_g_kv[0], a_w_ukv[0],
                          a_conv_w[0], a_conv_b[0], a_ln_g[0], a_ln_b[0], a_w_o[0])
    xa = _ffn(xa, mods[0], norm2_g[0], ffn_w1, ffn_w3, ffn_w2, 0, n_x)
    xl = _ssd_layer(xa, mods[1], n_x, norm1_g[1], m_w_in, m_conv_w[0], m_conv_b[0], m_dt_bias[0], m_a_log[0],
                    m_d[0], m_norm_g[0], m_w_out[0])
    xl = _ffn(xl, mods[1], norm2_g[1], ffn_w1, ffn_w3, ffn_w2, 1, xl.shape[0])
    return final_rmsnorm(xl, final_g)[None]
```
